```python
import math
import jax, jax.numpy as jnp
from jax import lax
import numpy as np

D_MODEL = 4096
BATCH = 4
SEQ = 2048
DEPTH = 2
DEC_BATCH = 8
DEC_SEQ = 8
PAST_LEN = 16384
PAGE_SIZE = 128

PLE_DIM = 256
CONV_A_DIM = 1024
CONV_A_WIDTH = 31
SSM_DIM = 1024
SSM_HEADDIM = 64
SSM_HEADS = SSM_DIM // SSM_HEADDIM
SSM_GROUPS = 4
SSM_STATE = 128
SSM_CONV = 4
SSM_CHUNK = 128
SSM_CONV_DIM = SSM_DIM + 2 * SSM_GROUPS * SSM_STATE
NSA_HEADS = 16
NSA_KV_HEADS = 4
GQA = NSA_HEADS // NSA_KV_HEADS
HEAD_DIM = 128
NSA_DIM = NSA_HEADS * HEAD_DIM
KV_DIM = NSA_KV_HEADS * HEAD_DIM
CMP_BLOCK = 32
SEL_BLOCK = 64
CMP_PER_SEL = SEL_BLOCK // CMP_BLOCK
SEL_TOPK = 16
WINDOW = 512
SWA_QBLOCK = 128
SEL_QBLOCK = 32
ROPE_THETA = 10000.0
D_MIX = CONV_A_DIM + SSM_DIM + NSA_DIM
D_FF = 11008
FFN_CONV = 3
EPS = 1e-6
NEG = -1e30
FORCE = 1e9
ATTN_SCALE = HEAD_DIM ** -0.5
IN_SIZES = (2 * CONV_A_DIM, SSM_DIM, SSM_CONV_DIM, SSM_HEADS, NSA_DIM, KV_DIM, KV_DIM, KV_DIM, KV_DIM, KV_DIM, KV_DIM, 3 * NSA_HEADS)
D_IN = sum(IN_SIZES)

kernel_name = 'hymba_conformer_ssd_nsa_decoder_step'


def rmsnorm(x, g):
    x32 = x.astype(jnp.float32)
    y = x32 * lax.rsqrt(jnp.mean(x32 * x32, axis=-1, keepdims=True) + EPS)
    return (y * g.astype(jnp.float32)).astype(x.dtype)


def layernorm(x, g, b):
    x32 = x.astype(jnp.float32)
    mu = jnp.mean(x32, axis=-1, keepdims=True)
    var = jnp.mean(jnp.square(x32 - mu), axis=-1, keepdims=True)
    y = (x32 - mu) * lax.rsqrt(var + EPS)
    return (y * g.astype(jnp.float32) + b.astype(jnp.float32)).astype(x.dtype)


def rope(x, pos):
    half = HEAD_DIM // 2
    inv = ROPE_THETA ** (-jnp.arange(half, dtype=jnp.float32) / half)
    ang = pos.astype(jnp.float32)[:, None] * inv[None, :]
    cos = jnp.cos(ang)[None, :, None, :]
    sin = jnp.sin(ang)[None, :, None, :]
    x1 = x[..., :half].astype(jnp.float32)
    x2 = x[..., half:].astype(jnp.float32)
    return jnp.concatenate([x1 * cos - x2 * sin, x1 * sin + x2 * cos], axis=-1).astype(x.dtype)


def split_cols(u):
    out, start = [], 0
    for size in IN_SIZES:
        out.append(u[..., start:start + size])
        start += size
    return out


def causal_dwconv(x, buf, w, b):
    width = w.shape[0]
    xp = jnp.concatenate([buf.astype(x.dtype), x], axis=1)
    y = lax.conv_general_dilated(xp, w[:, None, :].astype(x.dtype), window_strides=(1,), padding='VALID',
                                 dimension_numbers=('NWC', 'WIO', 'NWC'), feature_group_count=x.shape[-1])
    return y + b.astype(x.dtype), xp[:, xp.shape[1] - (width - 1):]


def conformer_conv(u, buf, w, b, ln_g, ln_b):
    glu = u[..., :CONV_A_DIM] * jax.nn.sigmoid(u[..., CONV_A_DIM:])
    c, new_buf = causal_dwconv(glu, buf, w, b)
    return jax.nn.silu(layernorm(c, ln_g, ln_b)), new_buf


def ssd_scan(x, dt, a, bm, cm, h0):
    f32 = jnp.float32
    bsz, t = x.shape[:2]
    chunk = min(SSM_CHUNK, t)
    n_chunks = -(-t // chunk)
    pad = n_chunks * chunk - t
    rep = SSM_HEADS // SSM_GROUPS
    xdt = x.astype(f32) * dt[..., None]
    bh = jnp.repeat(bm.astype(f32), rep, axis=2)
    ch = jnp.repeat(cm.astype(f32), rep, axis=2)
    la = dt * a

    def chunkify(z):
        z = jnp.pad(z, [(0, 0), (0, pad)] + [(0, 0)] * (z.ndim - 2))
        return z.reshape((bsz, n_chunks, chunk) + z.shape[2:])

    xdt, bh, ch, la = chunkify(xdt), chunkify(bh), chunkify(ch), chunkify(la)
    acum = jnp.cumsum(la, axis=2)
    seg = acum[:, :, :, None, :] - acum[:, :, None, :, :]
    causal = jnp.tril(jnp.ones((chunk, chunk), dtype=bool))[None, None, :, :, None]
    decay = jnp.exp(jnp.where(causal, seg, -jnp.inf))
    scores = jnp.einsum('bclhn,bcshn->bclsh', ch, bh) * decay
    y_diag = jnp.einsum('bclsh,bcshp->bclhp', scores, xdt)
    decay_end = jnp.exp(acum[:, :, -1:, :] - acum)
    chunk_states = jnp.einsum('bclhn,bclh,bclhp->bchpn', bh, decay_end, xdt)
    chunk_decay = jnp.exp(acum[:, :, -1, :])

    def step(h, inp):
        s_c, d_c = inp
        return d_c[:, :, None, None] * h + s_c, h

    h_final, h_prev = lax.scan(step, h0.astype(f32),
                               (jnp.moveaxis(chunk_states, 1, 0), jnp.moveaxis(chunk_decay, 1, 0)))
    h_prev = jnp.moveaxis(h_prev, 0, 1)
    y_off = jnp.einsum('bclhn,bchpn->bclhp', ch, h_prev) * jnp.exp(acum)[..., None]
    y = (y_diag + y_off).reshape(bsz, n_chunks * chunk, SSM_HEADS, SSM_HEADDIM)[:, :t]
    return y, h_final


def mamba2_mixer(z, xbc, dt_raw, conv_buf, h0, lp):
    bsz, t = z.shape[:2]
    gn = SSM_GROUPS * SSM_STATE
    xbc_c, new_buf = causal_dwconv(xbc, conv_buf, lp['ssm_conv_w'], lp['ssm_conv_b'])
    xbc_c = jax.nn.silu(xbc_c)
    xs = xbc_c[..., :SSM_DIM].reshape(bsz, t, SSM_HEADS, SSM_HEADDIM)
    bm = xbc_c[..., SSM_DIM:SSM_DIM + gn].reshape(bsz, t, SSM_GROUPS, SSM_STATE)
    cm = xbc_c[..., SSM_DIM + gn:].reshape(bsz, t, SSM_GROUPS, SSM_STATE)
    dt = jax.nn.softplus(dt_raw.astype(jnp.float32) + lp['ssm_dt_bias'].astype(jnp.float32))
    a = -jnp.exp(lp['ssm_a_log'].astype(jnp.float32))
    y, h_new = ssd_scan(xs, dt, a, bm, cm, h0)
    y = y + xs.astype(jnp.float32) * lp['ssm_d'].astype(jnp.float32)[:, None]
    y = y.reshape(bsz, t, SSM_DIM).astype(z.dtype) * jax.nn.silu(z)
    y = rmsnorm(y.reshape(bsz, t, SSM_GROUPS, SSM_DIM // SSM_GROUPS),
                lp['ssm_norm_g'].reshape(SSM_GROUPS, SSM_DIM // SSM_GROUPS)).reshape(bsz, t, SSM_DIM)
    return y, new_buf, h_new.astype(z.dtype)


def nsa_compress(k_full, v_full, pe, w_c):
    bsz, s = k_full.shape[:2]
    s_pad = -(-s // SEL_BLOCK) * SEL_BLOCK

    def comp(zz, pe_z, w_z):
        zz = jnp.pad(zz, ((0, 0), (0, s_pad - s), (0, 0), (0, 0)))
        zz = zz.reshape(bsz, s_pad // CMP_BLOCK, CMP_BLOCK, NSA_KV_HEADS, HEAD_DIM)
        m = jnp.mean(zz + pe_z[None, None, :, None, :], axis=2)
        return jnp.einsum('bnkd,de->bnke', m, w_z)

    return comp(k_full, pe[0], w_c[0]), comp(v_full, pe[1], w_c[1])


def cmp_attend(q, q_pos, k_c, v_c):
    bsz, tq = q.shape[:2]
    ncb = k_c.shape[1]
    qg = q.reshape(bsz, tq, NSA_KV_HEADS, GQA, HEAD_DIM)
    s = jnp.einsum('btkgd,bnkd->bkgtn', qg, k_c).astype(jnp.float32) * ATTN_SCALE
    blk_end = (jnp.arange(ncb) + 1) * CMP_BLOCK - 1
    vis = blk_end[None, :] <= q_pos[:, None]
    s = jnp.where(vis, s, NEG)
    p = jax.nn.softmax(s, axis=-1) * jnp.any(vis, axis=-1)[:, None].astype(jnp.float32)
    o = jnp.einsum('bkgtn,bnkd->btkgd', p.astype(v_c.dtype), v_c)
    return o.reshape(bsz, tq, NSA_HEADS, HEAD_DIM), p


def select_blocks(p_cmp, q_pos):
    bsz, _, _, tq, ncb = p_cmp.shape
    nsb = ncb // CMP_PER_SEL
    imp = p_cmp.sum(axis=2).reshape(bsz, NSA_KV_HEADS, tq, nsb, CMP_PER_SEL).sum(axis=-1)
    j = jnp.arange(nsb)[None, :]
    qp = q_pos[:, None]
    forced = (j == qp // SEL_BLOCK) | (j == 0)
    future = j * SEL_BLOCK > qp
    imp = jnp.where(forced, FORCE, jnp.where(future, NEG, imp))
    _, idx = lax.top_k(imp, min(SEL_TOPK, nsb))
    return idx


def sel_attend(q_r, q_pos, idx, kg, vg):
    bsz, tq = q_r.shape[:2]
    nk = idx.shape[-1]
    qg = q_r.reshape(bsz, tq, NSA_KV_HEADS, GQA, HEAD_DIM)
    s = jnp.einsum('btkgd,bktjsd->bkgtjs', qg, kg).astype(jnp.float32) * ATTN_SCALE
    kpos = idx[..., None] * SEL_BLOCK + jnp.arange(SEL_BLOCK)
    vis = kpos <= q_pos[None, None, :, None, None]
    s = jnp.where(vis[:, :, None], s, NEG)
    p = jax.nn.softmax(s.reshape(bsz, NSA_KV_HEADS, GQA, tq, nk * SEL_BLOCK), axis=-1).reshape(s.shape)
    o = jnp.einsum('bkgtjs,bktjsd->btkgd', p.astype(vg.dtype), vg)
    return o.reshape(bsz, tq, NSA_HEADS, HEAD_DIM)


def sel_prompt(q_r, idx, ks, vs):
    bsz, t = ks.shape[:2]
    nk = idx.shape[-1]
    kb = ks.reshape(bsz, t // SEL_BLOCK, SEL_BLOCK, NSA_KV_HEADS, HEAD_DIM)
    vb = vs.reshape(bsz, t // SEL_BLOCK, SEL_BLOCK, NSA_KV_HEADS, HEAD_DIM)
    bi = jnp.arange(bsz)[:, None, None, None]
    hi = jnp.arange(NSA_KV_HEADS)[None, :, None, None]
    nqb = t // SEL_QBLOCK
    qblk = q_r.reshape(bsz, nqb, SEL_QBLOCK, NSA_HEADS, HEAD_DIM).swapaxes(0, 1)
    iblk = idx.reshape(bsz, NSA_KV_HEADS, nqb, SEL_QBLOCK, nk).transpose(2, 0, 1, 3, 4)
    pblk = jnp.arange(t, dtype=jnp.int32).reshape(nqb, SEL_QBLOCK)

    def one(args):
        qb, ib, pb = args
        kg = kb[bi, ib, :, hi]
        vg = vb[bi, ib, :, hi]
        return sel_attend(qb, pb, ib, kg, vg)

    o = lax.map(one, (qblk, iblk, pblk))
    return o.swapaxes(0, 1).reshape(bsz, t, NSA_HEADS, HEAD_DIM)


def sel_sample(q_r, q_pos, idx, pool_k, pool_v, page_table, ks_new, vs_new):
    bsz, tq = ks_new.shape[:2]
    sub = PAGE_SIZE // SEL_BLOCK
    n_past_blk = page_table.shape[1] * sub
    pool_kb = pool_k.reshape(-1, SEL_BLOCK, NSA_KV_HEADS, HEAD_DIM)
    pool_vb = pool_v.reshape(-1, SEL_BLOCK, NSA_KV_HEADS, HEAD_DIM)
    new_pad = -(-tq // SEL_BLOCK) * SEL_BLOCK - tq
    padw = ((0, 0), (0, new_pad), (0, 0), (0, 0))
    new_kb = jnp.pad(ks_new, padw).reshape(bsz, -1, SEL_BLOCK, NSA_KV_HEADS, HEAD_DIM)
    new_vb = jnp.pad(vs_new, padw).reshape(bsz, -1, SEL_BLOCK, NSA_KV_HEADS, HEAD_DIM)
    bi = jnp.arange(bsz)[:, None, None, None]
    hi = jnp.arange(NSA_KV_HEADS)[None, :, None, None]
    past_j = jnp.minimum(idx, n_past_blk - 1)
    phys = page_table[bi, past_j // sub] * sub + past_j % sub
    new_j = jnp.clip(idx - n_past_blk, 0, new_kb.shape[1] - 1)
    in_past = (idx < n_past_blk)[..., None, None]
    kg = jnp.where(in_past, pool_kb[phys, :, hi], new_kb[bi, new_j, :, hi])
    vg = jnp.where(in_past, pool_vb[phys, :, hi], new_vb[bi, new_j, :, hi])
    return sel_attend(q_r, q_pos, idx, kg, vg)


def swa_attend(qb, qpos, kb, vb, kpos):
    bsz, n, nq = qb.shape[:3]
    qg = qb.reshape(bsz, n, nq, NSA_KV_HEADS, GQA, HEAD_DIM)
    s = jnp.einsum('bnqkgd,bnskd->bnkgqs', qg, kb).astype(jnp.float32) * ATTN_SCALE
    diff = qpos[:, :, None] - kpos[:, None, :]
    vis = (diff >= 0) & (diff <= WINDOW) & (kpos[:, None, :] >= 0)
    s = jnp.where(vis[None, :, None, None], s, NEG)
    p = jax.nn.softmax(s, axis=-1)
    o = jnp.einsum('bnkgqs,bnskd->bnqkgd', p.astype(vb.dtype), vb)
    return o.reshape(bsz, n * nq, NSA_HEADS, HEAD_DIM)


def swa_prompt(q_r, kw, vw):
    bsz, t = kw.shape[:2]
    nqb = t // SWA_QBLOCK
    span = WINDOW + SWA_QBLOCK
    padw = ((0, 0), (WINDOW, 0), (0, 0), (0, 0))
    kp = jnp.pad(kw, padw)
    vp = jnp.pad(vw, padw)
    gidx = jnp.arange(nqb)[:, None] * SWA_QBLOCK + jnp.arange(span)[None, :]
    qpos = jnp.arange(t, dtype=jnp.int32).reshape(nqb, SWA_QBLOCK)
    qb = q_r.reshape(bsz, nqb, SWA_QBLOCK, NSA_HEADS, HEAD_DIM)
    return swa_attend(qb, qpos, kp[:, gidx], vp[:, gidx], gidx - WINDOW)


def nsa_mixer(q, kc, vc, ks, vs, kw, vw, gates, q_pos, lp, st, win_buf):
    bsz, t = q.shape[:2]
    kvh = lambda zz: zz.reshape(bsz, t, NSA_KV_HEADS, HEAD_DIM)
    kn = lp['nsa_k_norm_g']
    q = rmsnorm(q.reshape(bsz, t, NSA_HEADS, HEAD_DIM), lp['nsa_q_norm_g'])
    q_r = rope(q, q_pos)
    kc = rmsnorm(kvh(kc), kn[0])
    vc = kvh(vc)
    ks = rope(rmsnorm(kvh(ks), kn[1]), q_pos)
    vs = kvh(vs)
    kw = rope(rmsnorm(kvh(kw), kn[2]), q_pos)
    vw = kvh(vw)
    if st is None:
        kc_full, vc_full = kc, vc
    else:
        pt = st['page_table']
        past_len = pt.shape[1] * PAGE_SIZE
        kc_full = jnp.concatenate([st['cmp_k'][pt].reshape(bsz, past_len, NSA_KV_HEADS, HEAD_DIM), kc], axis=1)
        vc_full = jnp.concatenate([st['cmp_v'][pt].reshape(bsz, past_len, NSA_KV_HEADS, HEAD_DIM), vc], axis=1)
    k_cmp, v_cmp = nsa_compress(kc_full, vc_full, lp['nsa_cmp_pe'], lp['nsa_cmp_w'])
    o_cmp, p_cmp = cmp_attend(q, q_pos, k_cmp, v_cmp)
    idx = select_blocks(p_cmp, q_pos)
    if st is None:
        o_sel = sel_prompt(q_r, idx, ks, vs)
        o_swa = swa_prompt(q_r, kw, vw)
        zpad = jnp.zeros((bsz, win_buf, NSA_KV_HEADS, HEAD_DIM), kw.dtype)
        kw_cat = jnp.concatenate([zpad, kw], axis=1)
        vw_cat = jnp.concatenate([zpad, vw], axis=1)
    else:
        o_sel = sel_sample(q_r, q_pos, idx, st['sel_k'], st['sel_v'], pt, ks, vs)
        kw_cat = jnp.concatenate([st['swa_k'].astype(kw.dtype), kw], axis=1)
        vw_cat = jnp.concatenate([st['swa_v'].astype(vw.dtype), vw], axis=1)
        kpos = past_len - win_buf + jnp.arange(win_buf + t, dtype=jnp.int32)
        o_swa = swa_attend(q_r[:, None], q_pos[None], kw_cat[:, None], vw_cat[:, None], kpos[None])
    g = jax.nn.sigmoid(gates.reshape(bsz, t, NSA_HEADS, 3))
    o = g[..., 0:1] * o_cmp + g[..., 1:2] * o_sel + g[..., 2:3] * o_swa
    return o.reshape(bsz, t, NSA_DIM), (kc, vc, ks, vs, kw_cat[:, t:], vw_cat[:, t:])


def trunk_layer(x, p_emb, pos, lp, st, win_buf):
    bsz, t, _ = x.shape
    h = rmsnorm(x, lp['attn_norm_g'])
    u = h @ lp['w_in']
    conv_in, ssm_z, ssm_xbc, ssm_dt, q, kc, vc, ks, vs, kw, vw, gates = split_cols(u)
    if st is None:
        buf_a = jnp.zeros((bsz, CONV_A_WIDTH - 1, CONV_A_DIM), x.dtype)
        buf_ssm = jnp.zeros((bsz, SSM_CONV - 1, SSM_CONV_DIM), x.dtype)
        h0 = jnp.zeros((bsz, SSM_HEADS, SSM_HEADDIM, SSM_STATE), jnp.float32)
        buf_f = jnp.zeros((bsz, FFN_CONV - 1, D_FF), x.dtype)
    else:
        buf_a, buf_ssm, h0, buf_f = st['conv_a'], st['ssm_conv'], st['ssm'], st['ffn_conv']
    a_out, new_a = conformer_conv(conv_in, buf_a, lp['conv_a_w'], lp['conv_a_b'], lp['conv_a_ln_g'], lp['conv_a_ln_b'])
    b_out, new_ssm_conv, new_h = mamba2_mixer(ssm_z, ssm_xbc, ssm_dt, buf_ssm, h0, lp)
    c_out, nsa_state = nsa_mixer(q, kc, vc, ks, vs, kw, vw, gates, pos, lp, st, win_buf)
    x = x + jnp.concatenate([a_out, b_out, c_out], axis=-1) @ lp['w_out']
    h2 = rmsnorm(x, lp['ffn_norm_g'])
    gu = h2 @ lp['w_up']
    gate_c, new_f = causal_dwconv(gu[..., :D_FF], buf_f, lp['ffn_conv_w'], lp['ffn_conv_b'])
    x = x + (jax.nn.silu(gate_c) * gu[..., D_FF:]) @ lp['w_down']
    ple_gate = jax.nn.sigmoid(rmsnorm(x, lp['ple_norm_g']) @ lp['w_ple_gate'])
    x = x + (p_emb @ lp['w_ple_proj']) * ple_gate
    return x, nsa_state + (new_h, new_ssm_conv, new_a, new_f)


def setup_inputs(seed: int = 0) -> dict:
    key = jax.random.key(seed)
    keys = iter(jax.random.split(key, 64))
    f32 = jnp.float32

    def normal(shape, scale=1.0):
        return scale * jax.random.normal(next(keys), shape, f32)

    def gain(shape):
        return 1.0 + normal(shape, 0.02)

    n_pages = PAST_LEN // PAGE_SIZE
    n_used = DEC_BATCH * n_pages
    n_phys = (n_used * 5) // 4
    win_buf = min(WINDOW, PAST_LEN)
    pool = (DEPTH, n_phys, PAGE_SIZE, NSA_KV_HEADS, HEAD_DIM)
    swa = (DEPTH, DEC_BATCH, win_buf, NSA_KV_HEADS, HEAD_DIM)
    page_table = jax.random.permutation(next(keys), n_phys)[:n_used].reshape(DEC_BATCH, n_pages).astype(jnp.int32)
    dt0 = jnp.exp(jax.random.uniform(next(keys), (DEPTH, SSM_HEADS), f32, math.log(1e-3), math.log(1e-1)))
    a_init = jax.random.uniform(next(keys), (DEPTH, SSM_HEADS), f32, 1.0, 16.0)
    return {
        'x_prompt': normal((BATCH, SEQ, D_MODEL)),
        'x_sample': normal((DEC_BATCH, DEC_SEQ, D_MODEL)),
        'cache_cmp_k': normal(pool),
        'cache_cmp_v': normal(pool),
        'cache_sel_k': normal(pool),
        'cache_sel_v': normal(pool),
        'state_swa_k': normal(swa),
        'state_swa_v': normal(swa),
        'state_ssm': normal((DEPTH, DEC_BATCH, SSM_HEADS, SSM_HEADDIM, SSM_STATE), 0.5),
        'state_ssm_conv': normal((DEPTH, DEC_BATCH, SSM_CONV - 1, SSM_CONV_DIM)),
        'state_conv_a': normal((DEPTH, DEC_BATCH, CONV_A_WIDTH - 1, CONV_A_DIM)),
        'state_ffn_conv': normal((DEPTH, DEC_BATCH, FFN_CONV - 1, D_FF)),
        'page_table': page_table,
        'p_prompt': normal((DEPTH, BATCH, SEQ, PLE_DIM)),
        'p_sample': normal((DEPTH, DEC_BATCH, DEC_SEQ, PLE_DIM)),
        'attn_norm_g': gain((DEPTH, D_MODEL)),
        'w_in': normal((DEPTH, D_MODEL, D_IN), D_MODEL ** -0.5),
        'conv_a_w': normal((DEPTH, CONV_A_WIDTH, CONV_A_DIM), CONV_A_WIDTH ** -0.5),
        'conv_a_b': normal((DEPTH, CONV_A_DIM), 0.02),
        'conv_a_ln_g': gain((DEPTH, CONV_A_DIM)),
        'conv_a_ln_b': normal((DEPTH, CONV_A_DIM), 0.02),
        'ssm_conv_w': normal((DEPTH, SSM_CONV, SSM_CONV_DIM), SSM_CONV ** -0.5),
        'ssm_conv_b': normal((DEPTH, SSM_CONV_DIM), 0.02),
        'ssm_dt_bias': dt0 + jnp.log(-jnp.expm1(-dt0)),
        'ssm_a_log': jnp.log(a_init),
        'ssm_d': gain((DEPTH, SSM_HEADS)),
        'ssm_norm_g': gain((DEPTH, SSM_DIM)),
        'nsa_q_norm_g': gain((DEPTH, HEAD_DIM)),
        'nsa_k_norm_g': gain((DEPTH, 3, HEAD_DIM)),
        'nsa_cmp_pe': normal((DEPTH, 2, CMP_BLOCK, HEAD_DIM), 0.1),
        'nsa_cmp_w': normal((DEPTH, 2, HEAD_DIM, HEAD_DIM), HEAD_DIM ** -0.5),
        'w_out': normal((DEPTH, D_MIX, D_MODEL), D_MIX ** -0.5),
        'ffn_norm_g': gain((DEPTH, D_MODEL)),
        'w_up': normal((DEPTH, D_MODEL, 2 * D_FF), D_MODEL ** -0.5),
        'ffn_conv_w': normal((DEPTH, FFN_CONV, D_FF), FFN_CONV ** -0.5),
        'ffn_conv_b': normal((DEPTH, D_FF), 0.02),
        'w_down': normal((DEPTH, D_FF, D_MODEL), D_FF ** -0.5),
        'ple_norm_g': gain((DEPTH, D_MODEL)),
        'w_ple_gate': normal((DEPTH, D_MODEL, D_MODEL), D_MODEL ** -0.5),
        'w_ple_proj': normal((DEPTH, PLE_DIM, D_MODEL), PLE_DIM ** -0.5),
    }


def reference(x_prompt, x_sample, cache_cmp_k, cache_cmp_v, cache_sel_k, cache_sel_v, state_swa_k, state_swa_v,
              state_ssm, state_ssm_conv, state_conv_a, state_ffn_conv, page_table, p_prompt, p_sample,
              attn_norm_g, w_in, conv_a_w, conv_a_b, conv_a_ln_g, conv_a_ln_b, ssm_conv_w, ssm_conv_b,
              ssm_dt_bias, ssm_a_log, ssm_d, ssm_norm_g, nsa_q_norm_g, nsa_k_norm_g, nsa_cmp_pe, nsa_cmp_w,
              w_out, ffn_norm_g, w_up, ffn_conv_w, ffn_conv_b, w_down, ple_norm_g, w_ple_gate, w_ple_proj):
    past_len = page_table.shape[1] * PAGE_SIZE
    win_buf = state_swa_k.shape[2]
    pos_p = jnp.arange(x_prompt.shape[1], dtype=jnp.int32)
    pos_s = past_len + jnp.arange(x_sample.shape[1], dtype=jnp.int32)
    y_p, y_s = x_prompt, x_sample
    states_p, states_s = [], []
    for i in range(DEPTH):
        lp = {'attn_norm_g': attn_norm_g[i], 'w_in': w_in[i], 'conv_a_w': conv_a_w[i], 'conv_a_b': conv_a_b[i],
              'conv_a_ln_g': conv_a_ln_g[i], 'conv_a_ln_b': conv_a_ln_b[i], 'ssm_conv_w': ssm_conv_w[i],
              'ssm_conv_b': ssm_conv_b[i], 'ssm_dt_bias': ssm_dt_bias[i], 'ssm_a_log': ssm_a_log[i],
              'ssm_d': ssm_d[i], 'ssm_norm_g': ssm_norm_g[i], 'nsa_q_norm_g': nsa_q_norm_g[i],
              'nsa_k_norm_g': nsa_k_norm_g[i], 'nsa_cmp_pe': nsa_cmp_pe[i], 'nsa_cmp_w': nsa_cmp_w[i],
              'w_out': w_out[i], 'ffn_norm_g': ffn_norm_g[i], 'w_up': w_up[i], 'ffn_conv_w': ffn_conv_w[i],
              'ffn_conv_b': ffn_conv_b[i], 'w_down': w_down[i], 'ple_norm_g': ple_norm_g[i],
              'w_ple_gate': w_ple_gate[i], 'w_ple_proj': w_ple_proj[i]}
        st = {'cmp_k': cache_cmp_k[i], 'cmp_v': cache_cmp_v[i], 'sel_k': cache_sel_k[i], 'sel_v': cache_sel_v[i],
              'swa_k': state_swa_k[i], 'swa_v': state_swa_v[i], 'ssm': state_ssm[i], 'ssm_conv': state_ssm_conv[i],
              'conv_a': state_conv_a[i], 'ffn_conv': state_ffn_conv[i], 'page_table': page_table}
        y_p, sp = trunk_layer(y_p, p_prompt[i], pos_p, lp, None, win_buf)
        y_s, ss = trunk_layer(y_s, p_sample[i], pos_s, lp, st, win_buf)
        states_p.append(sp)
        states_s.append(ss)
    (ck_p, cv_p, sk_p, sv_p, wk_p, wv_p, ssm_p, sc_p, ca_p, fc_p) = [jnp.stack(z) for z in zip(*states_p)]
    (ck_s, cv_s, sk_s, sv_s, wk_s, wv_s, ssm_s, sc_s, ca_s, fc_s) = [jnp.stack(z) for z in zip(*states_s)]
    return (y_p, y_s, ck_p, ck_s, cv_p, cv_s, sk_p, sk_s, sv_p, sv_s, wk_p, wk_s, wv_p, wv_s,
            ssm_p, ssm_s, sc_p, sc_s, ca_p, ca_s, fc_p, fc_s)
```

```python
import functools
import math

import jax
import jax.numpy as jnp
from jax import lax
from jax.experimental import pallas as pl
from jax.experimental.pallas import tpu as pltpu

D_MODEL = 4096
DEPTH = 2
PAGE_SIZE = 128
PLE_DIM = 256
CONV_A_DIM = 1024
CONV_A_WIDTH = 31
SSM_DIM = 1024
SSM_HEADDIM = 64
SSM_HEADS = SSM_DIM // SSM_HEADDIM
SSM_GROUPS = 4
SSM_STATE = 128
SSM_CONV = 4
SSM_CHUNK = 128
SSM_CONV_DIM = SSM_DIM + 2 * SSM_GROUPS * SSM_STATE
NSA_HEADS = 16
NSA_KV_HEADS = 4
GQA = NSA_HEADS // NSA_KV_HEADS
HEAD_DIM = 128
NSA_DIM = NSA_HEADS * HEAD_DIM
KV_DIM = NSA_KV_HEADS * HEAD_DIM
CMP_BLOCK = 32
SEL_BLOCK = 64
CMP_PER_SEL = SEL_BLOCK // CMP_BLOCK
SEL_TOPK = 16
WINDOW = 512
SWA_QBLOCK = 128
ROPE_THETA = 10000.0
D_FF = 11008
FFN_CONV = 3
EPS = 1e-6
NEG = -1e30
FORCE = 1e9
ATTN_SCALE = HEAD_DIM ** -0.5

OFF_CONV = 0
OFF_Z = OFF_CONV + 2 * CONV_A_DIM
OFF_XBC = OFF_Z + SSM_DIM
OFF_Q = OFF_XBC + SSM_CONV_DIM
OFF_KV = OFF_Q + NSA_DIM
U_MAIN = OFF_KV + 6 * KV_DIM
U_TAIL = 128
ORIG_DT = OFF_Q
ORIG_GATES = ORIG_DT + SSM_HEADS + NSA_DIM + 6 * KV_DIM

D_FF_PAD = 11264
VMEM_LIMIT = 48 * 1024 * 1024


def _cparams(sem):
    return pltpu.CompilerParams(dimension_semantics=sem, vmem_limit_bytes=VMEM_LIMIT)


def _rmsnorm_kernel(x_ref, g_ref, o_ref):
    x = x_ref[...]
    ms = jnp.mean(x * x, axis=-1, keepdims=True)
    o_ref[...] = (x * lax.rsqrt(ms + EPS) * g_ref[...]).astype(o_ref.dtype)


def _rmsnorm_bf16(x, g):
    m, d = x.shape
    tm = min(m, 256)
    return pl.pallas_call(
        _rmsnorm_kernel,
        grid=(m // tm,),
        in_specs=[pl.BlockSpec((tm, d), lambda i: (i, 0)), pl.BlockSpec((1, d), lambda i: (0, 0))],
        out_specs=pl.BlockSpec((tm, d), lambda i: (i, 0)),
        out_shape=jax.ShapeDtypeStruct((m, d), jnp.bfloat16),
        compiler_params=_cparams(("parallel",)),
        name="rmsnorm",
    )(x, g.reshape(1, d))


def _mm_kernel(a_ref, b_ref, *rest, nk, has_res):
    if has_res:
        r_ref, o_ref, acc_ref = rest
    else:
        o_ref, acc_ref = rest
    k = pl.program_id(2)
    part = jnp.dot(a_ref[...], b_ref[...], preferred_element_type=jnp.float32)

    if nk == 1:
        o_ref[...] = part + r_ref[...] if has_res else part
        return

    @pl.when(k == 0)
    def _():
        acc_ref[...] = part

    @pl.when(k > 0)
    def _():
        acc_ref[...] += part

    @pl.when(k == nk - 1)
    def _():
        o_ref[...] = acc_ref[...] + r_ref[...] if has_res else acc_ref[...]


def _matmul(a, b, residual=None, *, tm=1024, tn=512, tk=None):
    m, kdim = a.shape
    _, n = b.shape
    tm = min(tm, m)
    tn = min(tn, n)
    tk = kdim if tk is None else tk
    nk = kdim // tk
    in_specs = [pl.BlockSpec((tm, tk), lambda i, j, k: (i, k)), pl.BlockSpec((tk, tn), lambda i, j, k: (k, j))]
    args = [a, b]
    if residual is not None:
        in_specs.append(pl.BlockSpec((tm, tn), lambda i, j, k: (i, j)))
        args.append(residual)
    return pl.pallas_call(
        functools.partial(_mm_kernel, nk=nk, has_res=residual is not None),
        grid=(m // tm, n // tn, nk),
        in_specs=in_specs,
        out_specs=pl.BlockSpec((tm, tn), lambda i, j, k: (i, j)),
        out_shape=jax.ShapeDtypeStruct((m, n), jnp.float32),
        scratch_shapes=[pltpu.VMEM((tm, tn) if nk > 1 else (8, 128), jnp.float32)],
        compiler_params=_cparams(("parallel", "parallel", "arbitrary")),
        name="matmul",
    )(*args)


def _ple_kernel(h_ref, wg_ref, p_ref, wp_ref, x_ref, o_ref):
    gate = jax.nn.sigmoid(jnp.dot(h_ref[...], wg_ref[...], preferred_element_type=jnp.float32))
    proj = jnp.dot(p_ref[...], wp_ref[...], preferred_element_type=jnp.float32)
    o_ref[...] = x_ref[...] + proj * gate


def _ple(h, wg, p, wp, x, *, tm=1024, tn=512):
    m, d = h.shape
    n = wg.shape[1]
    tm = min(tm, m)
    return pl.pallas_call(
        _ple_kernel,
        grid=(m // tm, n // tn),
        in_specs=[pl.BlockSpec((tm, d), lambda i, j: (i, 0)), pl.BlockSpec((d, tn), lambda i, j: (0, j)),
                  pl.BlockSpec((tm, PLE_DIM), lambda i, j: (i, 0)), pl.BlockSpec((PLE_DIM, tn), lambda i, j: (0, j)),
                  pl.BlockSpec((tm, tn), lambda i, j: (i, j))],
        out_specs=pl.BlockSpec((tm, tn), lambda i, j: (i, j)),
        out_shape=jax.ShapeDtypeStruct((m, n), jnp.float32),
        compiler_params=_cparams(("parallel", "parallel")),
        name="ple",
    )(h, wg, p, wp, x)


def _rmsnorm(x, g):
    x32 = x.astype(jnp.float32)
    y = x32 * lax.rsqrt(jnp.mean(x32 * x32, axis=-1, keepdims=True) + EPS)
    return (y * g.astype(jnp.float32)).astype(x.dtype)


def _layernorm(x, g, b):
    mu = jnp.mean(x, axis=-1, keepdims=True)
    var = jnp.mean(jnp.square(x - mu), axis=-1, keepdims=True)
    return (x - mu) * lax.rsqrt(var + EPS) * g + b


def _rope(x, pos):
    half = HEAD_DIM // 2
    inv = ROPE_THETA ** (-jnp.arange(half, dtype=jnp.float32) / half)
    ang = pos.astype(jnp.float32)[:, None] * inv[None, :]
    cos = jnp.cos(ang)[None, :, None, :]
    sin = jnp.sin(ang)[None, :, None, :]
    x1 = x[..., :half]
    x2 = x[..., half:]
    return jnp.concatenate([x1 * cos - x2 * sin, x1 * sin + x2 * cos], axis=-1)


def _causal_dwconv(x, buf, w, b):
    width = w.shape[0]
    xp = jnp.concatenate([buf, x], axis=1)
    y = lax.conv_general_dilated(xp, w[:, None, :], window_strides=(1,), padding='VALID',
                                 dimension_numbers=('NWC', 'WIO', 'NWC'), feature_group_count=x.shape[-1])
    return y + b, xp[:, xp.shape[1] - (width - 1):]


def _conformer_conv(u, buf, w, b, ln_g, ln_b):
    glu = u[..., :CONV_A_DIM] * jax.nn.sigmoid(u[..., CONV_A_DIM:])
    c, new_buf = _causal_dwconv(glu, buf, w, b)
    return jax.nn.silu(_layernorm(c, ln_g, ln_b)), new_buf


def _ssd_scan(x, dt, a, bm, cm, h0):
    f32 = jnp.float32
    bsz, t = x.shape[:2]
    chunk = min(SSM_CHUNK, t)
    n_chunks = -(-t // chunk)
    pad = n_chunks * chunk - t
    rep = SSM_HEADS // SSM_GROUPS
    xdt = x * dt[..., None]
    bh = jnp.repeat(bm, rep, axis=2)
    ch = jnp.repeat(cm, rep, axis=2)
    la = dt * a

    def chunkify(z):
        z = jnp.pad(z, [(0, 0), (0, pad)] + [(0, 0)] * (z.ndim - 2))
        return z.reshape((bsz, n_chunks, chunk) + z.shape[2:])

    xdt, bh, ch, la = chunkify(xdt), chunkify(bh), chunkify(ch), chunkify(la)
    acum = jnp.cumsum(la, axis=2)
    seg = acum[:, :, :, None, :] - acum[:, :, None, :, :]
    causal = jnp.tril(jnp.ones((chunk, chunk), dtype=bool))[None, None, :, :, None]
    decay = jnp.exp(jnp.where(causal, seg, -jnp.inf))
    scores = jnp.einsum('bclhn,bcshn->bclsh', ch, bh) * decay
    y_diag = jnp.einsum('bclsh,bcshp->bclhp', scores, xdt)
    decay_end = jnp.exp(acum[:, :, -1:, :] - acum)
    chunk_states = jnp.einsum('bclhn,bclh,bclhp->bchpn', bh, decay_end, xdt)
    chunk_decay = jnp.exp(acum[:, :, -1, :])

    def step(h, inp):
        s_c, d_c = inp
        return d_c[:, :, None, None] * h + s_c, h

    h_final, h_prev = lax.scan(step, h0.astype(f32),
                               (jnp.moveaxis(chunk_states, 1, 0), jnp.moveaxis(chunk_decay, 1, 0)))
    h_prev = jnp.moveaxis(h_prev, 0, 1)
    y_off = jnp.einsum('bclhn,bchpn->bclhp', ch, h_prev) * jnp.exp(acum)[..., None]
    y = (y_diag + y_off).reshape(bsz, n_chunks * chunk, SSM_HEADS, SSM_HEADDIM)[:, :t]
    return y, h_final


def _mamba2_mixer(z, xbc, dt_raw, conv_buf, h0, lp):
    bsz, t = z.shape[:2]
    gn = SSM_GROUPS * SSM_STATE
    xbc_c, new_buf = _causal_dwconv(xbc, conv_buf, lp['ssm_conv_w'], lp['ssm_conv_b'])
    xbc_c = jax.nn.silu(xbc_c)
    xs = xbc_c[..., :SSM_DIM].reshape(bsz, t, SSM_HEADS, SSM_HEADDIM)
    bm = xbc_c[..., SSM_DIM:SSM_DIM + gn].reshape(bsz, t, SSM_GROUPS, SSM_STATE)
    cm = xbc_c[..., SSM_DIM + gn:].reshape(bsz, t, SSM_GROUPS, SSM_STATE)
    dt = jax.nn.softplus(dt_raw + lp['ssm_dt_bias'])
    a = -jnp.exp(lp['ssm_a_log'])
    y, h_new = _ssd_scan(xs, dt, a, bm, cm, h0)
    y = y + xs * lp['ssm_d'][:, None]
    y = y.reshape(bsz, t, SSM_DIM) * jax.nn.silu(z)
    y = _rmsnorm(y.reshape(bsz, t, SSM_GROUPS, SSM_DIM // SSM_GROUPS),
                 lp['ssm_norm_g'].reshape(SSM_GROUPS, SSM_DIM // SSM_GROUPS)).reshape(bsz, t, SSM_DIM)
    return y, new_buf, h_new


def _nsa_compress(k_full, v_full, pe, w_c):
    bsz, s = k_full.shape[:2]
    s_pad = -(-s // SEL_BLOCK) * SEL_BLOCK

    def comp(zz, pe_z, w_z):
        zz = jnp.pad(zz, ((0, 0), (0, s_pad - s), (0, 0), (0, 0)))
        zz = zz.reshape(bsz, s_pad // CMP_BLOCK, CMP_BLOCK, NSA_KV_HEADS, HEAD_DIM)
        m = jnp.mean(zz + pe_z[None, None, :, None, :], axis=2)
        return jnp.einsum('bnkd,de->bnke', m, w_z)

    return comp(k_full, pe[0], w_c[0]), comp(v_full, pe[1], w_c[1])


def _cmp_attend(q, q_pos, k_c, v_c):
    bsz, tq = q.shape[:2]
    ncb = k_c.shape[1]
    qg = q.reshape(bsz, tq, NSA_KV_HEADS, GQA, HEAD_DIM)
    s = jnp.einsum('btkgd,bnkd->bkgtn', qg, k_c) * ATTN_SCALE
    blk_end = (jnp.arange(ncb) + 1) * CMP_BLOCK - 1
    vis = blk_end[None, :] <= q_pos[:, None]
    s = jnp.where(vis, s, NEG)
    p = jax.nn.softmax(s, axis=-1) * jnp.any(vis, axis=-1)[:, None].astype(jnp.float32)
    o = jnp.einsum('bkgtn,bnkd->btkgd', p, v_c)
    return o.reshape(bsz, tq, NSA_HEADS, HEAD_DIM), p


def _select_blocks(p_cmp, q_pos):
    bsz, _, _, tq, ncb = p_cmp.shape
    nsb = ncb // CMP_PER_SEL
    imp = p_cmp.sum(axis=2).reshape(bsz, NSA_KV_HEADS, tq, nsb, CMP_PER_SEL).sum(axis=-1)
    j = jnp.arange(nsb)[None, :]
    qp = q_pos[:, None]
    forced = (j == qp // SEL_BLOCK) | (j == 0)
    future = j * SEL_BLOCK > qp
    imp = jnp.where(forced, FORCE, jnp.where(future, NEG, imp))
    _, idx = lax.top_k(imp, min(SEL_TOPK, nsb))
    return idx


def _sel_attend(q_r, q_pos, idx, kg, vg):
    bsz, tq = q_r.shape[:2]
    nk = idx.shape[-1]
    qg = q_r.reshape(bsz, tq, NSA_KV_HEADS, GQA, HEAD_DIM)
    s = jnp.einsum('btkgd,bktjsd->bkgtjs', qg, kg) * ATTN_SCALE
    kpos = idx[..., None] * SEL_BLOCK + jnp.arange(SEL_BLOCK)
    vis = kpos <= q_pos[None, None, :, None, None]
    s = jnp.where(vis[:, :, None], s, NEG)
    p = jax.nn.softmax(s.reshape(bsz, NSA_KV_HEADS, GQA, tq, nk * SEL_BLOCK), axis=-1).reshape(s.shape)
    o = jnp.einsum('bkgtjs,bktjsd->btkgd', p, vg)
    return o.reshape(bsz, tq, NSA_HEADS, HEAD_DIM)


def _sel_prompt_dense(q_r, idx, ks, vs):
    bsz, t = ks.shape[:2]
    nsb = t // SEL_BLOCK
    sel = jnp.any(idx[..., None] == jnp.arange(nsb), axis=-2)
    mask = jnp.repeat(sel, SEL_BLOCK, axis=-1) & (jnp.arange(t)[None, :] <= jnp.arange(t)[:, None])
    qg = q_r.reshape(bsz, t, NSA_KV_HEADS, GQA, HEAD_DIM)
    s = jnp.einsum('btkgd,bskd->bkgts', qg, ks) * ATTN_SCALE
    s = jnp.where(mask[:, :, None], s, NEG)
    p = jax.nn.softmax(s, axis=-1)
    o = jnp.einsum('bkgts,bskd->btkgd', p, vs)
    return o.reshape(bsz, t, NSA_HEADS, HEAD_DIM)


def _sel_sample(q_r, q_pos, idx, pool_k, pool_v, page_table, ks_new, vs_new):
    bsz, tq = ks_new.shape[:2]
    sub = PAGE_SIZE // SEL_BLOCK
    n_past_blk = page_table.shape[1] * sub
    pool_kb = pool_k.reshape(-1, SEL_BLOCK, NSA_KV_HEADS, HEAD_DIM)
    pool_vb = pool_v.reshape(-1, SEL_BLOCK, NSA_KV_HEADS, HEAD_DIM)
    new_pad = -(-tq // SEL_BLOCK) * SEL_BLOCK - tq
    padw = ((0, 0), (0, new_pad), (0, 0), (0, 0))
    new_kb = jnp.pad(ks_new, padw).reshape(bsz, -1, SEL_BLOCK, NSA_KV_HEADS, HEAD_DIM)
    new_vb = jnp.pad(vs_new, padw).reshape(bsz, -1, SEL_BLOCK, NSA_KV_HEADS, HEAD_DIM)
    bi = jnp.arange(bsz)[:, None, None, None]
    hi = jnp.arange(NSA_KV_HEADS)[None, :, None, None]
    past_j = jnp.minimum(idx, n_past_blk - 1)
    phys = page_table[bi, past_j // sub] * sub + past_j % sub
    new_j = jnp.clip(idx - n_past_blk, 0, new_kb.shape[1] - 1)
    in_past = (idx < n_past_blk)[..., None, None]
    kg = jnp.where(in_past, pool_kb[phys, :, hi], new_kb[bi, new_j, :, hi])
    vg = jnp.where(in_past, pool_vb[phys, :, hi], new_vb[bi, new_j, :, hi])
    return _sel_attend(q_r, q_pos, idx, kg, vg)


def _swa_attend(qb, qpos, kb, vb, kpos):
    bsz, n, nq = qb.shape[:3]
    qg = qb.reshape(bsz, n, nq, NSA_KV_HEADS, GQA, HEAD_DIM)
    s = jnp.einsum('bnqkgd,bnskd->bnkgqs', qg, kb) * ATTN_SCALE
    diff = qpos[:, :, None] - kpos[:, None, :]
    vis = (diff >= 0) & (diff <= WINDOW) & (kpos[:, None, :] >= 0)
    s = jnp.where(vis[None, :, None, None], s, NEG)
    p = jax.nn.softmax(s, axis=-1)
    o = jnp.einsum('bnkgqs,bnskd->bnqkgd', p, vb)
    return o.reshape(bsz, n * nq, NSA_HEADS, HEAD_DIM)


def _swa_prompt(q_r, kw, vw):
    bsz, t = kw.shape[:2]
    nqb = t // SWA_QBLOCK
    span = WINDOW + SWA_QBLOCK
    padw = ((0, 0), (WINDOW, 0), (0, 0), (0, 0))
    kp = jnp.pad(kw, padw)
    vp = jnp.pad(vw, padw)
    gidx = jnp.arange(nqb)[:, None] * SWA_QBLOCK + jnp.arange(span)[None, :]
    qpos = jnp.arange(t, dtype=jnp.int32).reshape(nqb, SWA_QBLOCK)
    qb = q_r.reshape(bsz, nqb, SWA_QBLOCK, NSA_HEADS, HEAD_DIM)
    return _swa_attend(qb, qpos, kp[:, gidx], vp[:, gidx], gidx - WINDOW)


def _nsa_mixer(q, kvs, gates, q_pos, lp, st, win_buf):
    bsz, t = q.shape[:2]
    kvh = lambda zz: zz.reshape(bsz, t, NSA_KV_HEADS, HEAD_DIM)
    kc, vc, ks, vs, kw, vw = [kvs[..., i * KV_DIM:(i + 1) * KV_DIM] for i in range(6)]
    kn = lp['nsa_k_norm_g']
    q = _rmsnorm(q.reshape(bsz, t, NSA_HEADS, HEAD_DIM), lp['nsa_q_norm_g'])
    q_r = _rope(q, q_pos)
    kc = _rmsnorm(kvh(kc), kn[0])
    vc = kvh(vc)
    ks = _rope(_rmsnorm(kvh(ks), kn[1]), q_pos)
    vs = kvh(vs)
    kw = _rope(_rmsnorm(kvh(kw), kn[2]), q_pos)
    vw = kvh(vw)
    if st is None:
        kc_full, vc_full = kc, vc
    else:
        pt = st['page_table']
        past_len = pt.shape[1] * PAGE_SIZE
        kc_full = jnp.concatenate([st['cmp_k'][pt].reshape(bsz, past_len, NSA_KV_HEADS, HEAD_DIM), kc], axis=1)
        vc_full = jnp.concatenate([st['cmp_v'][pt].reshape(bsz, past_len, NSA_KV_HEADS, HEAD_DIM), vc], axis=1)
    k_cmp, v_cmp = _nsa_compress(kc_full, vc_full, lp['nsa_cmp_pe'], lp['nsa_cmp_w'])
    o_cmp, p_cmp = _cmp_attend(q, q_pos, k_cmp, v_cmp)
    idx = _select_blocks(p_cmp, q_pos)
    if st is None:
        o_sel = _sel_prompt_dense(q_r, idx, ks, vs)
        o_swa = _swa_prompt(q_r, kw, vw)
        zpad = jnp.zeros((bsz, win_buf, NSA_KV_HEADS, HEAD_DIM), kw.dtype)
        kw_cat = jnp.concatenate([zpad, kw], axis=1)
        vw_cat = jnp.concatenate([zpad, vw], axis=1)
    else:
        o_sel = _sel_sample(q_r, q_pos, idx, st['sel_k'], st['sel_v'], pt, ks, vs)
        kw_cat = jnp.concatenate([st['swa_k'], kw], axis=1)
        vw_cat = jnp.concatenate([st['swa_v'], vw], axis=1)
        kpos = past_len - win_buf + jnp.arange(win_buf + t, dtype=jnp.int32)
        o_swa = _swa_attend(q_r[:, None], q_pos[None], kw_cat[:, None], vw_cat[:, None], kpos[None])
    g = jax.nn.sigmoid(gates.reshape(bsz, t, NSA_HEADS, 3))
    o = g[..., 0:1] * o_cmp + g[..., 1:2] * o_sel + g[..., 2:3] * o_swa
    return o.reshape(bsz, t, NSA_DIM), (kc, vc, ks, vs, kw_cat[:, t:], vw_cat[:, t:])


def _trunk_layer(x, p_emb, pos, lp, st, win_buf):
    bsz, t, _ = x.shape
    m = bsz * t
    bf16 = jnp.bfloat16
    x2 = x.reshape(m, D_MODEL)
    h = _rmsnorm_bf16(x2, lp['attn_norm_g'])
    u = _matmul(h, lp['w_in_main']).reshape(bsz, t, U_MAIN)
    u_tail = _matmul(h, lp['w_in_tail'], tn=U_TAIL).reshape(bsz, t, U_TAIL)
    if st is None:
        buf_a = jnp.zeros((bsz, CONV_A_WIDTH - 1, CONV_A_DIM), x.dtype)
        buf_ssm = jnp.zeros((bsz, SSM_CONV - 1, SSM_CONV_DIM), x.dtype)
        h0 = jnp.zeros((bsz, SSM_HEADS, SSM_HEADDIM, SSM_STATE), jnp.float32)
        buf_f = jnp.zeros((bsz, FFN_CONV - 1, D_FF), x.dtype)
    else:
        buf_a, buf_ssm, h0, buf_f = st['conv_a'], st['ssm_conv'], st['ssm'], st['ffn_conv']
    a_out, new_a = _conformer_conv(u[..., OFF_CONV:OFF_Z], buf_a, lp['conv_a_w'], lp['conv_a_b'],
                                   lp['conv_a_ln_g'], lp['conv_a_ln_b'])
    b_out, new_ssm_conv, new_h = _mamba2_mixer(u[..., OFF_Z:OFF_XBC], u[..., OFF_XBC:OFF_Q],
                                               u_tail[..., :SSM_HEADS], buf_ssm, h0, lp)
    c_out, nsa_state = _nsa_mixer(u[..., OFF_Q:OFF_KV], u[..., OFF_KV:U_MAIN],
                                  u_tail[..., SSM_HEADS:SSM_HEADS + 3 * NSA_HEADS], pos, lp, st, win_buf)
    mix = jnp.concatenate([a_out, b_out, c_out], axis=-1).reshape(m, D_MODEL).astype(bf16)
    x2 = _matmul(mix, lp['w_out'], residual=x2)
    h2 = _rmsnorm_bf16(x2, lp['ffn_norm_g'])
    gu = _matmul(h2, lp['w_up']).reshape(bsz, t, 2 * D_FF_PAD)
    gate_c, new_f = _causal_dwconv(gu[..., :D_FF], buf_f, lp['ffn_conv_w'], lp['ffn_conv_b'])
    act = jax.nn.silu(gate_c) * gu[..., D_FF_PAD:D_FF_PAD + D_FF]
    act = jnp.pad(act, ((0, 0), (0, 0), (0, D_FF_PAD - D_FF))).reshape(m, D_FF_PAD).astype(bf16)
    x2 = _matmul(act, lp['w_down'], residual=x2, tk=D_FF_PAD // 4)
    h3 = _rmsnorm_bf16(x2, lp['ple_norm_g'])
    x2 = _ple(h3, lp['w_ple_gate'], p_emb.reshape(m, PLE_DIM).astype(bf16), lp['w_ple_proj'], x2)
    return x2.reshape(bsz, t, D_MODEL), nsa_state + (new_h, new_ssm_conv, new_a, new_f)


def _prep_weights(w_in, w_out, w_up, w_down, w_ple_gate, w_ple_proj):
    bf16 = jnp.bfloat16
    w_in_main = jnp.concatenate([w_in[:, :ORIG_DT], w_in[:, ORIG_DT + SSM_HEADS:ORIG_GATES]], axis=1).astype(bf16)
    w_in_tail = jnp.concatenate([w_in[:, ORIG_DT:ORIG_DT + SSM_HEADS], w_in[:, ORIG_GATES:],
                                 jnp.zeros((D_MODEL, U_TAIL - SSM_HEADS - 3 * NSA_HEADS), w_in.dtype)], axis=1).astype(bf16)
    padc = ((0, 0), (0, D_FF_PAD - D_FF))
    w_up_p = jnp.concatenate([jnp.pad(w_up[:, :D_FF], padc), jnp.pad(w_up[:, D_FF:], padc)], axis=1).astype(bf16)
    w_down_p = jnp.pad(w_down, ((0, D_FF_PAD - D_FF), (0, 0))).astype(bf16)
    return dict(w_in_main=w_in_main, w_in_tail=w_in_tail, w_out=w_out.astype(bf16), w_up=w_up_p, w_down=w_down_p,
                w_ple_gate=w_ple_gate.astype(bf16), w_ple_proj=w_ple_proj.astype(bf16))


def kernel(x_prompt, x_sample, cache_cmp_k, cache_cmp_v, cache_sel_k, cache_sel_v, state_swa_k, state_swa_v,
           state_ssm, state_ssm_conv, state_conv_a, state_ffn_conv, page_table, p_prompt, p_sample,
           attn_norm_g, w_in, conv_a_w, conv_a_b, conv_a_ln_g, conv_a_ln_b, ssm_conv_w, ssm_conv_b,
           ssm_dt_bias, ssm_a_log, ssm_d, ssm_norm_g, nsa_q_norm_g, nsa_k_norm_g, nsa_cmp_pe, nsa_cmp_w,
           w_out, ffn_norm_g, w_up, ffn_conv_w, ffn_conv_b, w_down, ple_norm_g, w_ple_gate, w_ple_proj):
    past_len = page_table.shape[1] * PAGE_SIZE
    win_buf = state_swa_k.shape[2]
    pos_p = jnp.arange(x_prompt.shape[1], dtype=jnp.int32)
    pos_s = past_len + jnp.arange(x_sample.shape[1], dtype=jnp.int32)
    y_p, y_s = x_prompt, x_sample
    states_p, states_s = [], []
    for i in range(DEPTH):
        lp = {'attn_norm_g': attn_norm_g[i], 'conv_a_w': conv_a_w[i], 'conv_a_b': conv_a_b[i],
              'conv_a_ln_g': conv_a_ln_g[i], 'conv_a_ln_b': conv_a_ln_b[i], 'ssm_conv_w': ssm_conv_w[i],
              'ssm_conv_b': ssm_conv_b[i], 'ssm_dt_bias': ssm_dt_bias[i], 'ssm_a_log': ssm_a_log[i],
              'ssm_d': ssm_d[i], 'ssm_norm_g': ssm_norm_g[i], 'nsa_q_norm_g': nsa_q_norm_g[i],
              'nsa_k_norm_g': nsa_k_norm_g[i], 'nsa_cmp_pe': nsa_cmp_pe[i], 'nsa_cmp_w': nsa_cmp_w[i],
              'ffn_norm_g': ffn_norm_g[i], 'ffn_conv_w': ffn_conv_w[i],
              'ffn_conv_b': ffn_conv_b[i], 'ple_norm_g': ple_norm_g[i]}
        lp.update(_prep_weights(w_in[i], w_out[i], w_up[i], w_down[i], w_ple_gate[i], w_ple_proj[i]))
        st = {'cmp_k': cache_cmp_k[i], 'cmp_v': cache_cmp_v[i], 'sel_k': cache_sel_k[i], 'sel_v': cache_sel_v[i],
              'swa_k': state_swa_k[i], 'swa_v': state_swa_v[i], 'ssm': state_ssm[i], 'ssm_conv': state_ssm_conv[i],
              'conv_a': state_conv_a[i], 'ffn_conv': state_ffn_conv[i], 'page_table': page_table}
        y_p, sp = _trunk_layer(y_p, p_prompt[i], pos_p, lp, None, win_buf)
        y_s, ss = _trunk_layer(y_s, p_sample[i], pos_s, lp, st, win_buf)
        states_p.append(sp)
        states_s.append(ss)
    (ck_p, cv_p, sk_p, sv_p, wk_p, wv_p, ssm_p, sc_p, ca_p, fc_p) = [jnp.stack(z) for z in zip(*states_p)]
    (ck_s, cv_s, sk_s, sv_s, wk_s, wv_s, ssm_s, sc_s, ca_s, fc_s) = [jnp.stack(z) for z in zip(*states_s)]
    return (y_p, y_s, ck_p, ck_s, cv_p, cv_s, sk_p, sk_s, sv_p, sv_s, wk_p, wk_s, wv_p, wv_s,
            ssm_p, ssm_s, sc_p, sc_s, ca_p, ca_s, fc_p, fc_s)
```

```python
import functools

import jax
import jax.numpy as jnp
from jax import lax
from jax.experimental import pallas as pl
from jax.experimental.pallas import tpu as pltpu

D_MODEL = 4096
DEPTH = 2
PAGE_SIZE = 128
PLE_DIM = 256
CONV_A_DIM = 1024
CONV_A_WIDTH = 31
SSM_DIM = 1024
SSM_HEADDIM = 64
SSM_HEADS = SSM_DIM // SSM_HEADDIM
SSM_GROUPS = 4
SSM_STATE = 128
SSM_CONV = 4
SSM_CHUNK = 128
SSM_CONV_DIM = SSM_DIM + 2 * SSM_GROUPS * SSM_STATE
NSA_HEADS = 16
NSA_KV_HEADS = 4
GQA = NSA_HEADS // NSA_KV_HEADS
HEAD_DIM = 128
NSA_DIM = NSA_HEADS * HEAD_DIM
KV_DIM = NSA_KV_HEADS * HEAD_DIM
CMP_BLOCK = 32
SEL_BLOCK = 64
CMP_PER_SEL = SEL_BLOCK // CMP_BLOCK
SEL_TOPK = 16
WINDOW = 512
SWA_QBLOCK = 128
ROPE_THETA = 10000.0
D_FF = 11008
FFN_CONV = 3
EPS = 1e-6
NEG = -1e30
FORCE = 1e9
ATTN_SCALE = HEAD_DIM ** -0.5

LANES = 128
SUBLANES = 8

OFF_CONV = 0
OFF_XBC = OFF_CONV + 2 * CONV_A_DIM
OFF_Q = OFF_XBC + SSM_CONV_DIM
OFF_KV = OFF_Q + NSA_DIM
OFF_Z = OFF_KV + 6 * KV_DIM
U_MAIN = OFF_Z + SSM_DIM
U_TAIL = LANES
TAIL_GATES = SSM_HEADS
ORIG_Z = 2 * CONV_A_DIM
ORIG_XBC = ORIG_Z + SSM_DIM
ORIG_DT = ORIG_XBC + SSM_CONV_DIM
ORIG_Q = ORIG_DT + SSM_HEADS
ORIG_KV = ORIG_Q + NSA_DIM
ORIG_GATES = ORIG_KV + 6 * KV_DIM

D_FF_PAD = 11264
FFN_TC = 512
ROW_CHUNK = 256
CONV_A_HALO = 32
CONV_A_ROWS = 32
SSM_HALO = SUBLANES
NSA_TQ = 128
NSA_KC = 512
VMEM_LIMIT = 48 * 1024 * 1024

_NT = (((1,), (1,)), ((), ()))


def _cparams(sem):
    return pltpu.CompilerParams(dimension_semantics=sem, vmem_limit_bytes=VMEM_LIMIT)


def _dot(a, b):
    return jnp.dot(a, b, preferred_element_type=jnp.float32)


def _dot_nt(a, b):
    return lax.dot_general(a, b, _NT, preferred_element_type=jnp.float32)


def _dot_f32(a, b):
    return jnp.dot(a, b, preferred_element_type=jnp.float32, precision=lax.Precision.HIGHEST)


def _bf(x):
    return x.astype(jnp.bfloat16)


def _sigmoid(x):
    return jax.nn.sigmoid(x)


def _rmsnorm_kernel(x_ref, g_ref, o_ref):
    x = x_ref[...]
    ms = jnp.mean(x * x, axis=-1, keepdims=True)
    o_ref[...] = (x * lax.rsqrt(ms + EPS) * g_ref[...]).astype(o_ref.dtype)


def _rmsnorm_bf16(x, g):
    m, d = x.shape
    tm = min(m, 256)
    return pl.pallas_call(
        _rmsnorm_kernel,
        grid=(m // tm,),
        in_specs=[pl.BlockSpec((tm, d), lambda i: (i, 0)), pl.BlockSpec((1, d), lambda i: (0, 0))],
        out_specs=pl.BlockSpec((tm, d), lambda i: (i, 0)),
        out_shape=jax.ShapeDtypeStruct((m, d), jnp.bfloat16),
        compiler_params=_cparams(("parallel",)),
        name="rmsnorm",
    )(x, g.reshape(1, d))


def _mm_kernel(a_ref, b_ref, *rest, nk, has_res):
    if has_res:
        r_ref, o_ref, acc_ref = rest
    else:
        o_ref, acc_ref = rest
    k = pl.program_id(2)
    part = _dot(a_ref[...], b_ref[...])

    if nk == 1:
        o_ref[...] = part + r_ref[...] if has_res else part
        return

    @pl.when(k == 0)
    def _():
        acc_ref[...] = part

    @pl.when(k > 0)
    def _():
        acc_ref[...] += part

    @pl.when(k == nk - 1)
    def _():
        o_ref[...] = acc_ref[...] + r_ref[...] if has_res else acc_ref[...]


def _matmul(a, b, residual=None, *, tm=1024, tn=512, tk=None):
    m, kdim = a.shape
    _, n = b.shape
    tm = min(tm, m)
    tn = min(tn, n)
    tk = kdim if tk is None else tk
    nk = kdim // tk
    in_specs = [pl.BlockSpec((tm, tk), lambda i, j, k: (i, k)), pl.BlockSpec((tk, tn), lambda i, j, k: (k, j))]
    args = [a, b]
    if residual is not None:
        in_specs.append(pl.BlockSpec((tm, tn), lambda i, j, k: (i, j)))
        args.append(residual)
    return pl.pallas_call(
        functools.partial(_mm_kernel, nk=nk, has_res=residual is not None),
        grid=(m // tm, n // tn, nk),
        in_specs=in_specs,
        out_specs=pl.BlockSpec((tm, tn), lambda i, j, k: (i, j)),
        out_shape=jax.ShapeDtypeStruct((m, n), jnp.float32),
        scratch_shapes=[pltpu.VMEM((tm, tn) if nk > 1 else (SUBLANES, LANES), jnp.float32)],
        compiler_params=_cparams(("parallel", "parallel", "arbitrary")),
        name="matmul",
    )(*args)


def _ple_kernel(h_ref, wg_ref, p_ref, wp_ref, x_ref, o_ref):
    gate = _sigmoid(_dot(h_ref[...], wg_ref[...]))
    proj = _dot(p_ref[...], wp_ref[...])
    o_ref[...] = x_ref[...] + proj * gate


def _ple(h, wg, p, wp, x, *, tm=1024, tn=512):
    m, d = h.shape
    n = wg.shape[1]
    tm = min(tm, m)
    return pl.pallas_call(
        _ple_kernel,
        grid=(m // tm, n // tn),
        in_specs=[pl.BlockSpec((tm, d), lambda i, j: (i, 0)), pl.BlockSpec((d, tn), lambda i, j: (0, j)),
                  pl.BlockSpec((tm, PLE_DIM), lambda i, j: (i, 0)), pl.BlockSpec((PLE_DIM, tn), lambda i, j: (0, j)),
                  pl.BlockSpec((tm, tn), lambda i, j: (i, j))],
        out_specs=pl.BlockSpec((tm, tn), lambda i, j: (i, j)),
        out_shape=jax.ShapeDtypeStruct((m, n), jnp.float32),
        compiler_params=_cparams(("parallel", "parallel")),
        name="ple",
    )(h, wg, p, wp, x)


def _ffn_act_kernel(g_ref, u_ref, buf_ref, w_ref, b_ref, act_ref, nb_ref, *, t, rc):
    buf = buf_ref[0]
    w = w_ref[...]
    bias = b_ref[...]

    def body(ci, carry):
        r0 = pl.multiple_of(ci * rc, rc)
        cur = g_ref[0, pl.ds(r0, rc), :]
        p0 = pl.multiple_of(jnp.maximum(r0 - SUBLANES, 0), SUBLANES)
        prev = g_ref[0, pl.ds(p0, SUBLANES), :]
        first = ci == 0
        hm1 = jnp.where(first, buf[1:2], prev[SUBLANES - 1:SUBLANES])
        hm2 = jnp.where(first, buf[0:1], prev[SUBLANES - 2:SUBLANES - 1])
        row = lax.broadcasted_iota(jnp.int32, cur.shape, 0)
        g1 = jnp.where(row == 0, hm1, pltpu.roll(cur, 1, axis=0))
        g2 = jnp.where(row == 0, hm2, jnp.where(row == 1, hm1, pltpu.roll(cur, 2, axis=0)))
        c = w[0:1] * g2 + w[1:2] * g1 + w[2:3] * cur + bias
        act_ref[0, pl.ds(r0, rc), :] = (c * _sigmoid(c) * u_ref[0, pl.ds(r0, rc), :]).astype(act_ref.dtype)
        return carry

    lax.fori_loop(0, t // rc, body, 0)
    nb_ref[0] = g_ref[0, t - 2:t, :]


def _ffn_act(gu, buf, w, b, out_dtype):
    bsz, t, _ = gu.shape
    nc = D_FF_PAD // FFN_TC
    rc = min(t, ROW_CHUNK)
    return pl.pallas_call(
        functools.partial(_ffn_act_kernel, t=t, rc=rc),
        grid=(bsz, nc),
        in_specs=[pl.BlockSpec((1, t, FFN_TC), lambda bi, j: (bi, 0, j)),
                  pl.BlockSpec((1, t, FFN_TC), lambda bi, j: (bi, 0, j + nc)),
                  pl.BlockSpec((1, FFN_CONV - 1, FFN_TC), lambda bi, j: (bi, 0, j)),
                  pl.BlockSpec((FFN_CONV, FFN_TC), lambda bi, j: (0, j)),
                  pl.BlockSpec((1, FFN_TC), lambda bi, j: (0, j))],
        out_specs=[pl.BlockSpec((1, t, FFN_TC), lambda bi, j: (bi, 0, j)),
                   pl.BlockSpec((1, FFN_CONV - 1, FFN_TC), lambda bi, j: (bi, 0, j))],
        out_shape=[jax.ShapeDtypeStruct((bsz, t, D_FF_PAD), out_dtype),
                   jax.ShapeDtypeStruct((bsz, FFN_CONV - 1, D_FF_PAD), jnp.float32)],
        compiler_params=_cparams(("parallel", "parallel")),
        name="ffn_act",
    )(gu, gu, buf, w, b)


def _conformer_kernel(u_ref, buf_ref, w_ref, b_ref, lg_ref, lb_ref, o_ref, nb_ref, ext_ref, *, tt, rc):
    ti = pl.program_id(1)

    @pl.when(ti == 0)
    def _():
        ext_ref[0:CONV_A_HALO, :] = buf_ref[0]

    x = u_ref[0]
    ext_ref[CONV_A_HALO:CONV_A_HALO + tt, :] = x[:, :CONV_A_DIM] * _sigmoid(x[:, CONV_A_DIM:])
    bias = b_ref[...]
    lg = lg_ref[...]
    lb = lb_ref[...]
    first_tap = CONV_A_HALO - (CONV_A_WIDTH - 1)
    for r in range(tt // rc):
        r0 = r * rc
        acc = jnp.zeros((rc, CONV_A_DIM), jnp.float32) + bias
        for k in range(CONV_A_WIDTH):
            acc = acc + w_ref[k:k + 1, :] * ext_ref[r0 + first_tap + k:r0 + first_tap + k + rc, :]
        mu = jnp.mean(acc, axis=-1, keepdims=True)
        d = acc - mu
        var = jnp.mean(d * d, axis=-1, keepdims=True)
        y = d * lax.rsqrt(var + EPS) * lg + lb
        o_ref[0, r0:r0 + rc, :] = (y * _sigmoid(y)).astype(o_ref.dtype)
    carry = ext_ref[tt:tt + CONV_A_HALO, :]
    nb_ref[0] = carry
    ext_ref[0:CONV_A_HALO, :] = carry


def _conformer(u3, buf, w, b, lg, lb, out_dtype):
    bsz, t, _ = u3.shape
    tt = min(t, ROW_CHUNK)
    rc = min(tt, CONV_A_ROWS)
    vec = lambda: pl.BlockSpec((1, CONV_A_DIM), lambda bi, ti: (0, 0))
    return pl.pallas_call(
        functools.partial(_conformer_kernel, tt=tt, rc=rc),
        grid=(bsz, t // tt),
        in_specs=[pl.BlockSpec((1, tt, 2 * CONV_A_DIM), lambda bi, ti: (bi, ti, OFF_CONV // (2 * CONV_A_DIM))),
                  pl.BlockSpec((1, CONV_A_HALO, CONV_A_DIM), lambda bi, ti: (bi, 0, 0)),
                  pl.BlockSpec((CONV_A_WIDTH, CONV_A_DIM), lambda bi, ti: (0, 0)),
                  vec(), vec(), vec()],
        out_specs=[pl.BlockSpec((1, tt, CONV_A_DIM), lambda bi, ti: (bi, ti, 0)),
                   pl.BlockSpec((1, CONV_A_HALO, CONV_A_DIM), lambda bi, ti: (bi, 0, 0))],
        out_shape=[jax.ShapeDtypeStruct((bsz, t, CONV_A_DIM), out_dtype),
                   jax.ShapeDtypeStruct((bsz, CONV_A_HALO, CONV_A_DIM), jnp.float32)],
        scratch_shapes=[pltpu.VMEM((CONV_A_HALO + tt, CONV_A_DIM), jnp.float32)],
        compiler_params=_cparams(("parallel", "arbitrary")),
        name="conformer",
    )(u3, buf, w, b.reshape(1, -1), lg.reshape(1, -1), lb.reshape(1, -1))


def _ssd_kernel(z_ref, xbc_ref, dt_ref, cbuf_ref, h0_ref, cw_ref, cb_ref, dtb_ref, alog_ref, dskip_ref, ng_ref,
                y_ref, hout_ref, cbout_ref, ext_ref, h_ref, ys_ref, *, valid):
    f32 = jnp.float32
    ln = SSM_CHUNK
    ci = pl.program_id(1)

    @pl.when(ci == 0)
    def _():
        ext_ref[0:SSM_HALO, :] = cbuf_ref[0]
        h_ref[...] = h0_ref[0]

    ext_ref[SSM_HALO:SSM_HALO + ln, :] = xbc_ref[0]
    conv = jnp.zeros((ln, SSM_CONV_DIM), f32) + cb_ref[...]
    for k in range(SSM_CONV):
        s0 = SSM_HALO - (SSM_CONV - 1) + k
        conv = conv + cw_ref[k:k + 1, :] * ext_ref[s0:s0 + ln, :]
    c = conv * _sigmoid(conv)
    gn = SSM_GROUPS * SSM_STATE

    row = lax.broadcasted_iota(jnp.int32, (ln, LANES), 0)
    lane = lax.broadcasted_iota(jnp.int32, (ln, LANES), 1)
    xr = dt_ref[0] + dtb_ref[...]
    dt = jnp.maximum(xr, 0.0) + jnp.log1p(jnp.exp(-jnp.abs(xr)))
    if valid < ln:
        dt = jnp.where(row < valid, dt, 0.0)
    la = dt * (-jnp.exp(alog_ref[...]))
    tril = jnp.where(row >= lane, 1.0, 0.0).astype(f32)
    acum = _dot_f32(tril, la)
    acum_t = acum.T
    alast = acum[ln - 1:ln, :]
    causal = row >= lane
    lane_lo = lane < SSM_HEADDIM
    pair_w = 2 * SSM_HEADDIM

    cb_cache = {}
    for j in range(SSM_HEADS // 2):
        h0i, h1i = 2 * j, 2 * j + 1
        g = h0i // (SSM_HEADS // SSM_GROUPS)
        bm = _bf(c[:, SSM_DIM + g * SSM_STATE:SSM_DIM + (g + 1) * SSM_STATE])
        cm = _bf(c[:, SSM_DIM + gn + g * SSM_STATE:SSM_DIM + gn + (g + 1) * SSM_STATE])
        if g not in cb_cache:
            cb_cache[g] = _dot_nt(cm, bm)
        cbg = cb_cache[g]
        a0, a1 = acum[:, h0i:h0i + 1], acum[:, h1i:h1i + 1]
        dec0 = jnp.where(causal, jnp.exp(a0 - acum_t[h0i:h0i + 1, :]), 0.0)
        dec1 = jnp.where(causal, jnp.exp(a1 - acum_t[h1i:h1i + 1, :]), 0.0)
        sc = jnp.concatenate([_bf(cbg * dec0), _bf(cbg * dec1)], axis=1)
        xs = c[:, j * pair_w:(j + 1) * pair_w]
        xdt = xs * jnp.where(lane_lo, dt[:, h0i:h0i + 1], dt[:, h1i:h1i + 1])
        xblk = jnp.concatenate([_bf(jnp.where(lane_lo, xdt, 0.0)), _bf(jnp.where(lane_lo, 0.0, xdt))], axis=0)
        y_diag = _dot(sc, xblk)
        hp = h_ref[j * pair_w:(j + 1) * pair_w, :]
        y_off = _dot_nt(cm, _bf(hp)) * jnp.where(lane_lo, jnp.exp(a0), jnp.exp(a1))
        ys_ref[:, j * pair_w:(j + 1) * pair_w] = y_diag + y_off + xs * dskip_ref[:, j * pair_w:(j + 1) * pair_w]
        al0, al1 = alast[:, h0i:h0i + 1], alast[:, h1i:h1i + 1]
        dend = jnp.where(lane_lo, jnp.exp(al0 - a0), jnp.exp(al1 - a1))
        s_new = _dot(_bf((xdt * dend).T), bm)
        h_ref[j * pair_w:(j + 1) * pair_w, :] = jnp.where(row < SSM_HEADDIM, jnp.exp(al0), jnp.exp(al1)) * hp + s_new

    z = z_ref[0]
    y = ys_ref[...] * (z * _sigmoid(z))
    gw = SSM_DIM // SSM_GROUPS
    for g in range(SSM_GROUPS):
        yg = y[:, g * gw:(g + 1) * gw]
        ms = jnp.mean(yg * yg, axis=-1, keepdims=True)
        y_ref[0, :, g * gw:(g + 1) * gw] = (yg * lax.rsqrt(ms + EPS) * ng_ref[:, g * gw:(g + 1) * gw]).astype(y_ref.dtype)

    tail = ext_ref[valid:valid + SSM_HALO, :]
    cbout_ref[0] = tail
    ext_ref[0:SSM_HALO, :] = tail
    hout_ref[0] = h_ref[...]


def _ssd(u3, tail3, cbuf, h0, cw, cb, dtb, alog, dskip, ng, valid, out_dtype):
    bsz, t, _ = u3.shape
    ln = SSM_CHUNK
    full = lambda shape: pl.BlockSpec(shape, lambda bi, ci: (0,) * len(shape))
    return pl.pallas_call(
        functools.partial(_ssd_kernel, valid=valid),
        grid=(bsz, t // ln),
        in_specs=[pl.BlockSpec((1, ln, SSM_DIM), lambda bi, ci: (bi, ci, OFF_Z // SSM_DIM)),
                  pl.BlockSpec((1, ln, SSM_CONV_DIM), lambda bi, ci: (bi, ci, OFF_XBC // SSM_CONV_DIM)),
                  pl.BlockSpec((1, ln, U_TAIL), lambda bi, ci: (bi, ci, 0)),
                  pl.BlockSpec((1, SSM_HALO, SSM_CONV_DIM), lambda bi, ci: (bi, 0, 0)),
                  pl.BlockSpec((1, SSM_DIM, SSM_STATE), lambda bi, ci: (bi, 0, 0)),
                  full((SSM_CONV, SSM_CONV_DIM)), full((1, SSM_CONV_DIM)), full((1, LANES)), full((1, LANES)),
                  full((1, SSM_DIM)), full((1, SSM_DIM))],
        out_specs=[pl.BlockSpec((1, ln, SSM_DIM), lambda bi, ci: (bi, ci, 0)),
                   pl.BlockSpec((1, SSM_DIM, SSM_STATE), lambda bi, ci: (bi, 0, 0)),
                   pl.BlockSpec((1, SSM_HALO, SSM_CONV_DIM), lambda bi, ci: (bi, 0, 0))],
        out_shape=[jax.ShapeDtypeStruct((bsz, t, SSM_DIM), out_dtype),
                   jax.ShapeDtypeStruct((bsz, SSM_DIM, SSM_STATE), jnp.float32),
                   jax.ShapeDtypeStruct((bsz, SSM_HALO, SSM_CONV_DIM), jnp.float32)],
        scratch_shapes=[pltpu.VMEM((SSM_HALO + ln, SSM_CONV_DIM), jnp.float32),
                        pltpu.VMEM((SSM_DIM, SSM_STATE), jnp.float32),
                        pltpu.VMEM((ln, SSM_DIM), jnp.float32)],
        compiler_params=_cparams(("parallel", "arbitrary")),
        name="ssd",
    )(u3, u3, tail3, cbuf, h0, cw, cb, dtb, alog, dskip, ng)


def _nsa_prep_kernel(q_ref, kv_ref, cos_ref, sin_ref, qg_ref, kg_ref,
                     qn_ref, qr_ref, kc_ref, vc_ref, ks_ref, vs_ref, kw_ref, vw_ref):
    cos = cos_ref[...]
    sin = sin_ref[...]

    def norm(x, g):
        return x * lax.rsqrt(jnp.mean(x * x, axis=-1, keepdims=True) + EPS) * g

    def rope(x):
        return x * cos + pltpu.roll(x, HEAD_DIM // 2, axis=1) * sin

    qg = qg_ref[...]
    for h in range(NSA_HEADS):
        sl = slice(h * HEAD_DIM, (h + 1) * HEAD_DIM)
        x = norm(q_ref[:, sl], qg)
        qn_ref[:, sl] = x.astype(qn_ref.dtype)
        qr_ref[:, sl] = rope(x).astype(qr_ref.dtype)
    for h in range(NSA_KV_HEADS):
        sl = slice(h * HEAD_DIM, (h + 1) * HEAD_DIM)
        col = lambda i: slice(i * KV_DIM + h * HEAD_DIM, i * KV_DIM + (h + 1) * HEAD_DIM)
        kc_ref[:, sl] = norm(kv_ref[:, col(0)], kg_ref[0:1, :])
        vc_ref[:, sl] = kv_ref[:, col(1)]
        ks_ref[:, sl] = rope(norm(kv_ref[:, col(2)], kg_ref[1:2, :]))
        vs_ref[:, sl] = kv_ref[:, col(3)]
        kw_ref[:, sl] = rope(norm(kv_ref[:, col(4)], kg_ref[2:3, :]))
        vw_ref[:, sl] = kv_ref[:, col(5)]


def _nsa_prep(u, cos, sin, qg, kg):
    m = u.shape[0]
    tt = min(m, ROW_CHUNK)
    row = lambda w, blk=0: pl.BlockSpec((tt, w), lambda i: (i, blk))
    kv = jax.ShapeDtypeStruct((m, KV_DIM), jnp.float32)
    qo = jax.ShapeDtypeStruct((m, NSA_DIM), jnp.bfloat16)
    return pl.pallas_call(
        _nsa_prep_kernel,
        grid=(m // tt,),
        in_specs=[row(NSA_DIM, OFF_Q // NSA_DIM), row(6 * KV_DIM, OFF_KV // (6 * KV_DIM)), row(HEAD_DIM), row(HEAD_DIM),
                  pl.BlockSpec((1, HEAD_DIM), lambda i: (0, 0)), pl.BlockSpec((3, HEAD_DIM), lambda i: (0, 0))],
        out_specs=[row(NSA_DIM), row(NSA_DIM)] + [row(KV_DIM)] * 6,
        out_shape=[qo, qo] + [kv] * 6,
        compiler_params=_cparams(("parallel",)),
        name="nsa_prep",
    )(u, u, cos, sin, qg.reshape(1, HEAD_DIM), kg)


def _softmax_rows(s):
    m = jnp.max(s, axis=-1, keepdims=True)
    e = jnp.exp(s - m)
    return e / jnp.sum(e, axis=-1, keepdims=True)


def _nsa_attn_kernel(qn_ref, qr_ref, tail_ref, kc_ref, vc_ref, ks_ref, vs_ref, kw_ref, vw_ref, pe_ref, cw_ref,
                     o_ref, kcmp_ref, kcmpp_ref, vcmp_ref, ksb_ref, vsb_ref, kwb_ref, vwb_ref,
                     m_ref, l_ref, acc_ref, *, t, tq):
    f32 = jnp.float32
    kvh = pl.program_id(1)
    qi = pl.program_id(2)
    ncb = t // CMP_BLOCK
    nsb = t // SEL_BLOCK
    rows = GQA * tq
    span = min(WINDOW + SWA_QBLOCK, t)
    cmp_shift = CMP_BLOCK.bit_length() - 1
    sel_shift = SEL_BLOCK.bit_length() - 1

    @pl.when(qi == 0)
    def _():
        r = lax.broadcasted_iota(jnp.int32, (ncb, t), 0)
        cblk = lax.broadcasted_iota(jnp.int32, (ncb, t), 1) >> cmp_shift
        avg_nat = jnp.where(cblk == r, 1.0 / CMP_BLOCK, 0.0).astype(f32)
        perm = jnp.where(r < nsb, 2 * r, 2 * (r - nsb) + 1)
        avg_perm = jnp.where(cblk == perm, 1.0 / CMP_BLOCK, 0.0).astype(f32)
        kc = kc_ref[...]
        pe_k = jnp.mean(pe_ref[0], axis=0, keepdims=True)
        pe_v = jnp.mean(pe_ref[1], axis=0, keepdims=True)
        wk = _bf(cw_ref[0])
        wv = _bf(cw_ref[1])
        kcmp_ref[...] = _bf(_dot(_bf(_dot_f32(avg_nat, kc) + pe_k), wk))
        kcmpp_ref[...] = _bf(_dot(_bf(_dot_f32(avg_perm, kc) + pe_k), wk))
        vcmp_ref[...] = _bf(_dot(_bf(_dot_f32(avg_nat, vc_ref[...]) + pe_v), wv))
        ksb_ref[...] = _bf(ks_ref[...])
        vsb_ref[...] = _bf(vs_ref[...])
        kwb_ref[...] = _bf(kw_ref[...])
        vwb_ref[...] = _bf(vw_ref[...])

    t0 = qi * tq

    def stack(ref):
        return jnp.concatenate([ref[:, g * HEAD_DIM:(g + 1) * HEAD_DIM] for g in range(GQA)], axis=0)

    qn = stack(qn_ref)
    qr = stack(qr_ref)

    tpos_c = t0 + (lax.broadcasted_iota(jnp.int32, (rows, ncb), 0) & (tq - 1))
    blk_end = (lax.broadcasted_iota(jnp.int32, (rows, ncb), 1) + 1) * CMP_BLOCK - 1
    s = jnp.where(blk_end <= tpos_c, _dot_nt(qn, kcmp_ref[...]) * ATTN_SCALE, NEG)
    anyvis = jnp.where(tpos_c[:, 0:1] >= CMP_BLOCK - 1, 1.0, 0.0).astype(f32)
    p = _softmax_rows(s) * anyvis
    o_cmp = _dot(_bf(p), vcmp_ref[...])

    rperm = lax.broadcasted_iota(jnp.int32, (ncb, rows), 0)
    blk_t = jnp.where(rperm < nsb, 2 * rperm, 2 * (rperm - nsb) + 1)
    tpos_t = t0 + (lax.broadcasted_iota(jnp.int32, (ncb, rows), 1) & (tq - 1))
    st = jnp.where((blk_t + 1) * CMP_BLOCK - 1 <= tpos_t, _dot_nt(kcmpp_ref[...], qn) * ATTN_SCALE, NEG)
    mt = jnp.max(st, axis=0, keepdims=True)
    et = jnp.exp(st - mt)
    pt = et / jnp.sum(et, axis=0, keepdims=True) * jnp.where(tpos_t[0:1, :] >= CMP_BLOCK - 1, 1.0, 0.0).astype(f32)
    psum = pt[:, 0:tq]
    for g in range(1, GQA):
        psum = psum + pt[:, g * tq:(g + 1) * tq]
    imp = psum[0:nsb, :] + psum[nsb:2 * nsb, :]
    jrow = lax.broadcasted_iota(jnp.int32, (nsb, tq), 0)
    qp = t0 + lax.broadcasted_iota(jnp.int32, (nsb, tq), 1)
    forced = (jrow == (qp >> sel_shift)) | (jrow == 0)
    imp = jnp.where(forced, FORCE, jnp.where(jrow * SEL_BLOCK > qp, NEG, imp))
    cnt = jnp.zeros((nsb, tq), f32)
    for i in range(nsb):
        ri = imp[i:i + 1, :]
        cnt = cnt + jnp.where((ri > imp) | ((ri == imp) & (jrow > i)), 1.0, 0.0)
    sel_t = jnp.where(cnt < SEL_TOPK, 1.0, 0.0).astype(f32)
    sel_pad = jnp.concatenate([sel_t, jnp.zeros((LANES - nsb, tq), f32)], axis=0) if nsb < LANES else sel_t
    sel_q = _bf(sel_pad.T)

    kc_sz = min(NSA_KC, t)
    m_ref[...] = jnp.full((rows, 1), NEG, f32)
    l_ref[...] = jnp.zeros((rows, 1), f32)
    acc_ref[...] = jnp.zeros((rows, HEAD_DIM), f32)
    qpos_s = t0 + lax.broadcasted_iota(jnp.int32, (tq, kc_sz), 0)

    def sel_body(ck, carry):
        k0 = pl.multiple_of(ck * kc_sz, kc_sz)
        kb = ksb_ref[pl.ds(k0, kc_sz), :]
        vb = vsb_ref[pl.ds(k0, kc_sz), :]
        kpos = k0 + lax.broadcasted_iota(jnp.int32, (tq, kc_sz), 1)
        eblk = (k0 + lax.broadcasted_iota(jnp.int32, (LANES, kc_sz), 1)) >> sel_shift
        expand = jnp.where(eblk == lax.broadcasted_iota(jnp.int32, (LANES, kc_sz), 0), 1.0, 0.0).astype(jnp.bfloat16)
        chosen = _dot(sel_q, expand)
        bias1 = jnp.where((chosen > 0.5) & (kpos <= qpos_s), 0.0, NEG).astype(f32)
        bias = jnp.concatenate([bias1] * GQA, axis=0)
        sc = _dot_nt(qr, kb) * ATTN_SCALE
        sc = jnp.where(bias < 0.0, NEG, sc)
        m_old = m_ref[...]
        m_new = jnp.maximum(m_old, jnp.max(sc, axis=-1, keepdims=True))
        alpha = jnp.exp(m_old - m_new)
        pc = jnp.exp(sc - m_new)
        l_ref[...] = alpha * l_ref[...] + jnp.sum(pc, axis=-1, keepdims=True)
        acc_ref[...] = alpha * acc_ref[...] + _dot(_bf(pc), vb)
        m_ref[...] = m_new
        return carry

    lax.fori_loop(0, (t0 + tq + kc_sz - 1) // kc_sz, sel_body, 0)
    o_sel = acc_ref[...] / l_ref[...]

    ws = pl.multiple_of(jnp.maximum(t0 + tq - span, 0), SWA_QBLOCK)
    kb = kwb_ref[pl.ds(ws, span), :]
    vb = vwb_ref[pl.ds(ws, span), :]
    diff = (t0 + (lax.broadcasted_iota(jnp.int32, (rows, span), 0) & (tq - 1))) - (
        ws + lax.broadcasted_iota(jnp.int32, (rows, span), 1))
    sw = jnp.where((diff >= 0) & (diff <= WINDOW), _dot_nt(qr, kb) * ATTN_SCALE, NEG)
    o_swa = _dot(_bf(_softmax_rows(sw)), vb)

    sg = _sigmoid(tail_ref[...])
    glane = lax.broadcasted_iota(jnp.int32, (tq, U_TAIL), 1)

    def gate(branch):
        cols = []
        for g in range(GQA):
            want = TAIL_GATES + (kvh * GQA + g) * 3 + branch
            cols.append(jnp.sum(jnp.where(glane == want, sg, 0.0), axis=-1, keepdims=True))
        return jnp.concatenate(cols, axis=0)

    o = gate(0) * o_cmp + gate(1) * o_sel + gate(2) * o_swa
    for g in range(GQA):
        o_ref[:, g * HEAD_DIM:(g + 1) * HEAD_DIM] = o[g * tq:(g + 1) * tq, :].astype(o_ref.dtype)


def _nsa_attn_prompt(qn, qr, tail, kc, vc, ks, vs, kw, vw, pe, cw, bsz, t):
    tq = NSA_TQ
    nq = t // tq
    ncb = t // CMP_BLOCK
    rows = GQA * tq
    qspec = pl.BlockSpec((tq, GQA * HEAD_DIM), lambda b, k, q: (b * nq + q, k))
    kvspec = pl.BlockSpec((t, HEAD_DIM), lambda b, k, q: (b, k))
    bf16 = jnp.bfloat16
    return pl.pallas_call(
        functools.partial(_nsa_attn_kernel, t=t, tq=tq),
        grid=(bsz, NSA_KV_HEADS, nq),
        in_specs=[qspec, qspec, pl.BlockSpec((tq, U_TAIL), lambda b, k, q: (b * nq + q, 0))] + [kvspec] * 6 + [
            pl.BlockSpec((2, CMP_BLOCK, HEAD_DIM), lambda b, k, q: (0, 0, 0)),
            pl.BlockSpec((2, HEAD_DIM, HEAD_DIM), lambda b, k, q: (0, 0, 0))],
        out_specs=qspec,
        out_shape=jax.ShapeDtypeStruct((bsz * t, NSA_DIM), bf16),
        scratch_shapes=[pltpu.VMEM((ncb, HEAD_DIM), bf16), pltpu.VMEM((ncb, HEAD_DIM), bf16),
                        pltpu.VMEM((ncb, HEAD_DIM), bf16)] + [pltpu.VMEM((t, HEAD_DIM), bf16)] * 4 + [
                        pltpu.VMEM((rows, 1), jnp.float32), pltpu.VMEM((rows, 1), jnp.float32),
                        pltpu.VMEM((rows, HEAD_DIM), jnp.float32)],
        compiler_params=_cparams(("parallel", "parallel", "arbitrary")),
        name="nsa_attn",
    )(qn, qr, tail, kc, vc, ks, vs, kw, vw, pe, cw)


def _nsa_compress(k_full, v_full, pe, w_c):
    bsz, s = k_full.shape[:2]
    s_pad = -(-s // SEL_BLOCK) * SEL_BLOCK

    def comp(zz, pe_z, w_z):
        zz = jnp.pad(zz, ((0, 0), (0, s_pad - s), (0, 0), (0, 0)))
        zz = zz.reshape(bsz, s_pad // CMP_BLOCK, CMP_BLOCK, NSA_KV_HEADS, HEAD_DIM)
        m = jnp.mean(zz + pe_z[None, None, :, None, :], axis=2)
        return jnp.einsum('bnkd,de->bnke', m, w_z)

    return comp(k_full, pe[0], w_c[0]), comp(v_full, pe[1], w_c[1])


def _cmp_attend(q, q_pos, k_c, v_c):
    bsz, tq = q.shape[:2]
    ncb = k_c.shape[1]
    qg = q.reshape(bsz, tq, NSA_KV_HEADS, GQA, HEAD_DIM)
    s = jnp.einsum('btkgd,bnkd->bkgtn', qg, k_c) * ATTN_SCALE
    blk_end = (jnp.arange(ncb) + 1) * CMP_BLOCK - 1
    vis = blk_end[None, :] <= q_pos[:, None]
    s = jnp.where(vis, s, NEG)
    p = jax.nn.softmax(s, axis=-1) * jnp.any(vis, axis=-1)[:, None].astype(jnp.float32)
    o = jnp.einsum('bkgtn,bnkd->btkgd', p, v_c)
    return o.reshape(bsz, tq, NSA_HEADS, HEAD_DIM), p


def _select_blocks(p_cmp, q_pos):
    bsz, _, _, tq, ncb = p_cmp.shape
    nsb = ncb // CMP_PER_SEL
    imp = p_cmp.sum(axis=2).reshape(bsz, NSA_KV_HEADS, tq, nsb, CMP_PER_SEL).sum(axis=-1)
    j = jnp.arange(nsb)[None, :]
    qp = q_pos[:, None]
    forced = (j == qp // SEL_BLOCK) | (j == 0)
    future = j * SEL_BLOCK > qp
    imp = jnp.where(forced, FORCE, jnp.where(future, NEG, imp))
    _, idx = lax.top_k(imp, min(SEL_TOPK, nsb))
    return idx


def _sel_attend(q_r, q_pos, idx, kg, vg):
    bsz, tq = q_r.shape[:2]
    nk = idx.shape[-1]
    qg = q_r.reshape(bsz, tq, NSA_KV_HEADS, GQA, HEAD_DIM)
    s = jnp.einsum('btkgd,bktjsd->bkgtjs', qg, kg) * ATTN_SCALE
    kpos = idx[..., None] * SEL_BLOCK + jnp.arange(SEL_BLOCK)
    vis = kpos <= q_pos[None, None, :, None, None]
    s = jnp.where(vis[:, :, None], s, NEG)
    p = jax.nn.softmax(s.reshape(bsz, NSA_KV_HEADS, GQA, tq, nk * SEL_BLOCK), axis=-1).reshape(s.shape)
    o = jnp.einsum('bkgtjs,bktjsd->btkgd', p, vg)
    return o.reshape(bsz, tq, NSA_HEADS, HEAD_DIM)


def _sel_sample(q_r, q_pos, idx, pool_k, pool_v, page_table, ks_new, vs_new):
    bsz, tq = ks_new.shape[:2]
    sub = PAGE_SIZE // SEL_BLOCK
    n_past_blk = page_table.shape[1] * sub
    pool_kb = pool_k.reshape(-1, SEL_BLOCK, NSA_KV_HEADS, HEAD_DIM)
    pool_vb = pool_v.reshape(-1, SEL_BLOCK, NSA_KV_HEADS, HEAD_DIM)
    new_pad = -(-tq // SEL_BLOCK) * SEL_BLOCK - tq
    padw = ((0, 0), (0, new_pad), (0, 0), (0, 0))
    new_kb = jnp.pad(ks_new, padw).reshape(bsz, -1, SEL_BLOCK, NSA_KV_HEADS, HEAD_DIM)
    new_vb = jnp.pad(vs_new, padw).reshape(bsz, -1, SEL_BLOCK, NSA_KV_HEADS, HEAD_DIM)
    bi = jnp.arange(bsz)[:, None, None, None]
    hi = jnp.arange(NSA_KV_HEADS)[None, :, None, None]
    past_j = jnp.minimum(idx, n_past_blk - 1)
    phys = page_table[bi, past_j // sub] * sub + past_j % sub
    new_j = jnp.clip(idx - n_past_blk, 0, new_kb.shape[1] - 1)
    in_past = (idx < n_past_blk)[..., None, None]
    kg = jnp.where(in_past, pool_kb[phys, :, hi], new_kb[bi, new_j, :, hi])
    vg = jnp.where(in_past, pool_vb[phys, :, hi], new_vb[bi, new_j, :, hi])
    return _sel_attend(q_r, q_pos, idx, kg, vg)


def _swa_attend(qb, qpos, kb, vb, kpos):
    bsz, n, nq = qb.shape[:3]
    qg = qb.reshape(bsz, n, nq, NSA_KV_HEADS, GQA, HEAD_DIM)
    s = jnp.einsum('bnqkgd,bnskd->bnkgqs', qg, kb) * ATTN_SCALE
    diff = qpos[:, :, None] - kpos[:, None, :]
    vis = (diff >= 0) & (diff <= WINDOW) & (kpos[:, None, :] >= 0)
    s = jnp.where(vis[None, :, None, None], s, NEG)
    p = jax.nn.softmax(s, axis=-1)
    o = jnp.einsum('bnkgqs,bnskd->bnqkgd', p, vb)
    return o.reshape(bsz, n * nq, NSA_HEADS, HEAD_DIM)


def _nsa_sample(qn, qr, kc, vc, ks, vs, kw, vw, gates, q_pos, lp, st):
    bsz, t = qn.shape[:2]
    pt = st['page_table']
    past_len = pt.shape[1] * PAGE_SIZE
    win_buf = st['swa_k'].shape[1]
    kc_full = jnp.concatenate([st['cmp_k'][pt].reshape(bsz, past_len, NSA_KV_HEADS, HEAD_DIM), kc], axis=1)
    vc_full = jnp.concatenate([st['cmp_v'][pt].reshape(bsz, past_len, NSA_KV_HEADS, HEAD_DIM), vc], axis=1)
    k_cmp, v_cmp = _nsa_compress(kc_full, vc_full, lp['nsa_cmp_pe'], lp['nsa_cmp_w'])
    o_cmp, p_cmp = _cmp_attend(qn, q_pos, k_cmp, v_cmp)
    idx = _select_blocks(p_cmp, q_pos)
    o_sel = _sel_sample(qr, q_pos, idx, st['sel_k'], st['sel_v'], pt, ks, vs)
    kw_cat = jnp.concatenate([st['swa_k'], kw], axis=1)
    vw_cat = jnp.concatenate([st['swa_v'], vw], axis=1)
    kpos = past_len - win_buf + jnp.arange(win_buf + t, dtype=jnp.int32)
    o_swa = _swa_attend(qr[:, None], q_pos[None], kw_cat[:, None], vw_cat[:, None], kpos[None])
    g = jax.nn.sigmoid(gates.reshape(bsz, t, NSA_HEADS, 3))
    o = g[..., 0:1] * o_cmp + g[..., 1:2] * o_sel + g[..., 2:3] * o_swa
    return o.reshape(bsz, t, NSA_DIM), kw_cat[:, t:], vw_cat[:, t:]


def _rope_tables(pos, bsz):
    half = HEAD_DIM // 2
    inv = ROPE_THETA ** (-jnp.arange(half, dtype=jnp.float32) / half)
    ang = pos.astype(jnp.float32)[:, None] * inv[None, :]
    cos = jnp.cos(ang)
    sin = jnp.sin(ang)
    cos2 = jnp.concatenate([cos, cos], axis=-1)
    sin2 = jnp.concatenate([-sin, sin], axis=-1)
    return jnp.tile(cos2, (bsz, 1)), jnp.tile(sin2, (bsz, 1))


def _trunk_layer(x, p_emb, pos, lp, st, win_buf):
    bsz, t, _ = x.shape
    m = bsz * t
    bf16, f32 = jnp.bfloat16, jnp.float32
    prompt = st is None
    act_dtype = bf16 if prompt else f32
    x2 = x.reshape(m, D_MODEL)
    h = _rmsnorm_bf16(x2, lp['attn_norm_g'])
    u = _matmul(h, lp['w_in_main'])
    tail = _matmul(h, lp['w_in_tail'], tn=U_TAIL)
    u3 = u.reshape(bsz, t, U_MAIN)
    tail3 = tail.reshape(bsz, t, U_TAIL)
    if prompt:
        buf_a = jnp.zeros((bsz, CONV_A_HALO, CONV_A_DIM), f32)
        buf_ssm = jnp.zeros((bsz, SSM_HALO, SSM_CONV_DIM), f32)
        h0 = jnp.zeros((bsz, SSM_DIM, SSM_STATE), f32)
        buf_f = jnp.zeros((bsz, FFN_CONV - 1, D_FF_PAD), f32)
    else:
        buf_a = jnp.pad(st['conv_a'], ((0, 0), (CONV_A_HALO - (CONV_A_WIDTH - 1), 0), (0, 0)))
        buf_ssm = jnp.pad(st['ssm_conv'], ((0, 0), (SSM_HALO - (SSM_CONV - 1), 0), (0, 0)))
        h0 = st['ssm'].reshape(bsz, SSM_DIM, SSM_STATE)
        buf_f = jnp.pad(st['ffn_conv'], ((0, 0), (0, 0), (0, D_FF_PAD - D_FF)))

    a_out, nb_a = _conformer(u3, buf_a, lp['conv_a_w'], lp['conv_a_b'], lp['conv_a_ln_g'], lp['conv_a_ln_b'], act_dtype)
    new_a = nb_a[:, CONV_A_HALO - (CONV_A_WIDTH - 1):]

    if t % SSM_CHUNK:
        padt = ((0, 0), (0, SSM_CHUNK - t), (0, 0))
        u3s, tail3s, valid = jnp.pad(u3, padt), jnp.pad(tail3, padt), t
    else:
        u3s, tail3s, valid = u3, tail3, SSM_CHUNK
    pad_h = lambda v: jnp.pad(v.reshape(1, SSM_HEADS), ((0, 0), (0, LANES - SSM_HEADS)))
    b_out, h_new, nb_s = _ssd(u3s, tail3s, buf_ssm, h0, lp['ssm_conv_w'], lp['ssm_conv_b'].reshape(1, -1),
                              pad_h(lp['ssm_dt_bias']), pad_h(lp['ssm_a_log']),
                              jnp.repeat(lp['ssm_d'], SSM_HEADDIM).reshape(1, SSM_DIM),
                              lp['ssm_norm_g'].reshape(1, SSM_DIM), valid, bf16 if prompt else f32)
    b_out = b_out[:, :t]
    new_h = h_new.reshape(bsz, SSM_HEADS, SSM_HEADDIM, SSM_STATE)
    new_ssm_conv = nb_s[:, SSM_HALO - (SSM_CONV - 1):]

    cos, sin = _rope_tables(pos, bsz)
    qn, qr, kc, vc, ks, vs, kw, vw = _nsa_prep(u, cos, sin, lp['nsa_q_norm_g'], lp['nsa_k_norm_g'])
    kv4 = lambda v: v.reshape(bsz, t, NSA_KV_HEADS, HEAD_DIM)
    if prompt:
        c_out = _nsa_attn_prompt(qn, qr, tail, kc, vc, ks, vs, kw, vw, lp['nsa_cmp_pe'], lp['nsa_cmp_w'], bsz, t)
        c_out = c_out.reshape(bsz, t, NSA_DIM)
        zpad = jnp.zeros((bsz, win_buf, NSA_KV_HEADS, HEAD_DIM), f32)
        kw_new = jnp.concatenate([zpad, kv4(kw)], axis=1)[:, t:]
        vw_new = jnp.concatenate([zpad, kv4(vw)], axis=1)[:, t:]
    else:
        q4 = lambda v: v.astype(f32).reshape(bsz, t, NSA_HEADS, HEAD_DIM)
        gates = tail3[..., TAIL_GATES:TAIL_GATES + 3 * NSA_HEADS]
        c_out, kw_new, vw_new = _nsa_sample(q4(qn), q4(qr), kv4(kc), kv4(vc), kv4(ks), kv4(vs), kv4(kw), kv4(vw),
                                            gates, pos, lp, st)
    nsa_state = (kv4(kc), kv4(vc), kv4(ks), kv4(vs), kw_new, vw_new)

    mix = jnp.concatenate([a_out.astype(bf16), b_out.astype(bf16), c_out.astype(bf16)], axis=-1).reshape(m, D_MODEL)
    x2 = _matmul(mix, lp['w_out'], residual=x2)
    h2 = _rmsnorm_bf16(x2, lp['ffn_norm_g'])
    gu = _matmul(h2, lp['w_up']).reshape(bsz, t, 2 * D_FF_PAD)
    act, nb_f = _ffn_act(gu, buf_f, lp['ffn_conv_w'], lp['ffn_conv_b'], act_dtype)
    new_f = nb_f[:, :, :D_FF]
    x2 = _matmul(act.reshape(m, D_FF_PAD).astype(bf16), lp['w_down'], residual=x2, tk=D_FF_PAD // 4)
    h3 = _rmsnorm_bf16(x2, lp['ple_norm_g'])
    x2 = _ple(h3, lp['w_ple_gate'], p_emb.reshape(m, PLE_DIM).astype(bf16), lp['w_ple_proj'], x2)
    return x2.reshape(bsz, t, D_MODEL), nsa_state + (new_h, new_ssm_conv, new_a, new_f)


def _prep_weights(w_in, w_out, w_up, w_down, w_ple_gate, w_ple_proj, ffn_conv_w, ffn_conv_b):
    bf16 = jnp.bfloat16
    w_in_main = jnp.concatenate([w_in[:, :ORIG_Z], w_in[:, ORIG_XBC:ORIG_DT], w_in[:, ORIG_Q:ORIG_GATES],
                                 w_in[:, ORIG_Z:ORIG_XBC]], axis=1).astype(bf16)
    w_in_tail = jnp.concatenate([w_in[:, ORIG_DT:ORIG_Q], w_in[:, ORIG_GATES:],
                                 jnp.zeros((D_MODEL, U_TAIL - SSM_HEADS - 3 * NSA_HEADS), w_in.dtype)], axis=1).astype(bf16)
    padc = ((0, 0), (0, D_FF_PAD - D_FF))
    w_up_p = jnp.concatenate([jnp.pad(w_up[:, :D_FF], padc), jnp.pad(w_up[:, D_FF:], padc)], axis=1).astype(bf16)
    w_down_p = jnp.pad(w_down, ((0, D_FF_PAD - D_FF), (0, 0))).astype(bf16)
    return dict(w_in_main=w_in_main, w_in_tail=w_in_tail, w_out=w_out.astype(bf16), w_up=w_up_p, w_down=w_down_p,
                w_ple_gate=w_ple_gate.astype(bf16), w_ple_proj=w_ple_proj.astype(bf16),
                ffn_conv_w=jnp.pad(ffn_conv_w, padc), ffn_conv_b=jnp.pad(ffn_conv_b.reshape(1, D_FF), padc))


def kernel(x_prompt, x_sample, cache_cmp_k, cache_cmp_v, cache_sel_k, cache_sel_v, state_swa_k, state_swa_v,
           state_ssm, state_ssm_conv, state_conv_a, state_ffn_conv, page_table, p_prompt, p_sample,
           attn_norm_g, w_in, conv_a_w, conv_a_b, conv_a_ln_g, conv_a_ln_b, ssm_conv_w, ssm_conv_b,
           ssm_dt_bias, ssm_a_log, ssm_d, ssm_norm_g, nsa_q_norm_g, nsa_k_norm_g, nsa_cmp_pe, nsa_cmp_w,
           w_out, ffn_norm_g, w_up, ffn_conv_w, ffn_conv_b, w_down, ple_norm_g, w_ple_gate, w_ple_proj):
    past_len = page_table.shape[1] * PAGE_SIZE
    win_buf = state_swa_k.shape[2]
    pos_p = jnp.arange(x_prompt.shape[1], dtype=jnp.int32)
    pos_s = past_len + jnp.arange(x_sample.shape[1], dtype=jnp.int32)
    y_p, y_s = x_prompt, x_sample
    states_p, states_s = [], []
    for i in range(DEPTH):
        lp = {'attn_norm_g': attn_norm_g[i], 'conv_a_w': conv_a_w[i], 'conv_a_b': conv_a_b[i],
              'conv_a_ln_g': conv_a_ln_g[i], 'conv_a_ln_b': conv_a_ln_b[i], 'ssm_conv_w': ssm_conv_w[i],
              'ssm_conv_b': ssm_conv_b[i], 'ssm_dt_bias': ssm_dt_bias[i], 'ssm_a_log': ssm_a_log[i],
              'ssm_d': ssm_d[i], 'ssm_norm_g': ssm_norm_g[i], 'nsa_q_norm_g': nsa_q_norm_g[i],
              'nsa_k_norm_g': nsa_k_norm_g[i], 'nsa_cmp_pe': nsa_cmp_pe[i], 'nsa_cmp_w': nsa_cmp_w[i],
              'ffn_norm_g': ffn_norm_g[i], 'ple_norm_g': ple_norm_g[i]}
        lp.update(_prep_weights(w_in[i], w_out[i], w_up[i], w_down[i], w_ple_gate[i], w_ple_proj[i],
                                ffn_conv_w[i], ffn_conv_b[i]))
        st = {'cmp_k': cache_cmp_k[i], 'cmp_v': cache_cmp_v[i], 'sel_k': cache_sel_k[i], 'sel_v': cache_sel_v[i],
              'swa_k': state_swa_k[i], 'swa_v': state_swa_v[i], 'ssm': state_ssm[i], 'ssm_conv': state_ssm_conv[i],
              'conv_a': state_conv_a[i], 'ffn_conv': state_ffn_conv[i], 'page_table': page_table}
        y_p, sp = _trunk_layer(y_p, p_prompt[i], pos_p, lp, None, win_buf)
        y_s, ss = _trunk_layer(y_s, p_sample[i], pos_s, lp, st, win_buf)
        states_p.append(sp)
        states_s.append(ss)
    (ck_p, cv_p, sk_p, sv_p, wk_p, wv_p, ssm_p, sc_p, ca_p, fc_p) = [jnp.stack(z) for z in zip(*states_p)]
    (ck_s, cv_s, sk_s, sv_s, wk_s, wv_s, ssm_s, sc_s, ca_s, fc_s) = [jnp.stack(z) for z in zip(*states_s)]
    return (y_p, y_s, ck_p, ck_s, cv_p, cv_s, sk_p, sk_s, sv_p, sv_s, wk_p, wk_s, wv_p, wv_s,
            ssm_p, ssm_s, sc_p, sc_s, ca_p, ca_s, fc_p, fc_s)
```

```python
import functools

import jax
import jax.numpy as jnp
from jax import lax
from jax.experimental import pallas as pl
from jax.experimental.pallas import tpu as pltpu

D_MODEL = 4096
DEPTH = 2
PAGE_SIZE = 128
PLE_DIM = 256
CONV_A_DIM = 1024
CONV_A_WIDTH = 31
SSM_DIM = 1024
SSM_HEADDIM = 64
SSM_HEADS = SSM_DIM // SSM_HEADDIM
SSM_GROUPS = 4
SSM_STATE = 128
SSM_CONV = 4
SSM_CHUNK = 128
SSM_CONV_DIM = SSM_DIM + 2 * SSM_GROUPS * SSM_STATE
NSA_HEADS = 16
NSA_KV_HEADS = 4
GQA = NSA_HEADS // NSA_KV_HEADS
HEAD_DIM = 128
NSA_DIM = NSA_HEADS * HEAD_DIM
KV_DIM = NSA_KV_HEADS * HEAD_DIM
CMP_BLOCK = 32
SEL_BLOCK = 64
CMP_PER_SEL = SEL_BLOCK // CMP_BLOCK
SEL_TOPK = 16
WINDOW = 512
SWA_QBLOCK = 128
ROPE_THETA = 10000.0
D_FF = 11008
FFN_CONV = 3
EPS = 1e-6
NEG = -1e30
FORCE = 1e9
ATTN_SCALE = HEAD_DIM ** -0.5

LANES = 128
SUBLANES = 8

OFF_CONV = 0
OFF_Z = OFF_CONV + 2 * CONV_A_DIM
OFF_XBC = OFF_Z + SSM_DIM
U_A = OFF_XBC + SSM_CONV_DIM
OFF_Q = 0
OFF_KV = OFF_Q + NSA_DIM
U_B = OFF_KV + 6 * KV_DIM
U_TAIL = LANES
TAIL_GATES = SSM_HEADS
ORIG_DT = U_A
ORIG_Q = ORIG_DT + SSM_HEADS
ORIG_GATES = ORIG_Q + U_B

D_FF_HALF = D_FF // 2
FFN_TC = 256
ROW_CHUNK = 256
CONV_A_HALO = 32
CONV_A_ROWS = 32
SSM_HALO = SUBLANES
NSA_TQ = 256
NSA_KC = 512
VMEM_LIMIT = 54 * 1024 * 1024

_NT = (((1,), (1,)), ((), ()))


def _cparams(sem):
    return pltpu.CompilerParams(dimension_semantics=sem, vmem_limit_bytes=VMEM_LIMIT)


def _dot(a, b):
    return jnp.dot(a, b, preferred_element_type=jnp.float32)


def _dot_nt(a, b):
    return lax.dot_general(a, b, _NT, preferred_element_type=jnp.float32)


def _dot_f32(a, b):
    return jnp.dot(a, b, preferred_element_type=jnp.float32, precision=lax.Precision.HIGHEST)


def _bf(x):
    return x.astype(jnp.bfloat16)


def _sigmoid(x):
    return jax.nn.sigmoid(x)


def _rmsnorm_kernel(x_ref, g_ref, o_ref):
    x = x_ref[...]
    ms = jnp.mean(x * x, axis=-1, keepdims=True)
    o_ref[...] = (x * lax.rsqrt(ms + EPS) * g_ref[...]).astype(o_ref.dtype)


def _rmsnorm_bf16(x, g):
    m, d = x.shape
    tm = min(m, 256)
    return pl.pallas_call(
        _rmsnorm_kernel,
        grid=(m // tm,),
        in_specs=[pl.BlockSpec((tm, d), lambda i: (i, 0)), pl.BlockSpec((1, d), lambda i: (0, 0))],
        out_specs=pl.BlockSpec((tm, d), lambda i: (i, 0)),
        out_shape=jax.ShapeDtypeStruct((m, d), jnp.bfloat16),
        compiler_params=_cparams(("parallel",)),
        name="rmsnorm",
    )(x, g.reshape(1, d))


def _mm_kernel(a_ref, b_ref, *rest, nk, has_res):
    if has_res:
        r_ref, o_ref, acc_ref = rest
    else:
        o_ref, acc_ref = rest
    k = pl.program_id(2)
    part = _dot(a_ref[...], b_ref[...])

    if nk == 1:
        o_ref[...] = part + r_ref[...] if has_res else part
        return

    @pl.when(k == 0)
    def _():
        acc_ref[...] = part

    @pl.when(k > 0)
    def _():
        acc_ref[...] += part

    @pl.when(k == nk - 1)
    def _():
        o_ref[...] = acc_ref[...] + r_ref[...] if has_res else acc_ref[...]


def _matmul(a, b, residual=None, *, tm=1024, tn=512, tk=None):
    m, kdim = a.shape
    _, n = b.shape
    tm = min(tm, m)
    tn = min(tn, n)
    tk = kdim if tk is None else tk
    nk = kdim // tk
    in_specs = [pl.BlockSpec((tm, tk), lambda i, j, k: (i, k)), pl.BlockSpec((tk, tn), lambda i, j, k: (k, j))]
    args = [a, b]
    if residual is not None:
        in_specs.append(pl.BlockSpec((tm, tn), lambda i, j, k: (i, j)))
        args.append(residual)
    return pl.pallas_call(
        functools.partial(_mm_kernel, nk=nk, has_res=residual is not None),
        grid=(m // tm, n // tn, nk),
        in_specs=in_specs,
        out_specs=pl.BlockSpec((tm, tn), lambda i, j, k: (i, j)),
        out_shape=jax.ShapeDtypeStruct((m, n), jnp.float32),
        scratch_shapes=[pltpu.VMEM((tm, tn) if nk > 1 else (SUBLANES, LANES), jnp.float32)],
        compiler_params=_cparams(("parallel", "parallel", "arbitrary")),
        name="matmul",
    )(*args)


def _mmw_kernel(a_ref, w_ref, *rest, has_res):
    if has_res:
        r_ref, o_ref, wb_ref = rest
    else:
        o_ref, wb_ref = rest

    @pl.when(pl.program_id(1) == 0)
    def _():
        wb_ref[...] = _bf(w_ref[...])

    part = _dot(a_ref[...], wb_ref[...])
    o_ref[...] = part + r_ref[...] if has_res else part


def _matmul_w(a, w, layer, residual=None, *, n_cols=None, tm=1024, tn=512, tk=None, kblk=0):
    m = a.shape[0]
    tk = w.shape[1] if tk is None else tk
    n = w.shape[2] if n_cols is None else n_cols
    tm = min(tm, m)
    in_specs = [pl.BlockSpec((tm, tk), lambda j, i: (i, kblk)),
                pl.BlockSpec((None, tk, tn), lambda j, i: (layer, kblk, j))]
    args = [a, w]
    if residual is not None:
        in_specs.append(pl.BlockSpec((tm, tn), lambda j, i: (i, j)))
        args.append(residual)
    return pl.pallas_call(
        functools.partial(_mmw_kernel, has_res=residual is not None),
        grid=(n // tn, m // tm),
        in_specs=in_specs,
        out_specs=pl.BlockSpec((tm, tn), lambda j, i: (i, j)),
        out_shape=jax.ShapeDtypeStruct((m, n), jnp.float32),
        scratch_shapes=[pltpu.VMEM((tk, tn), jnp.bfloat16)],
        compiler_params=_cparams(("parallel", "arbitrary")),
        name="matmul_w",
    )(*args)


def _ple_kernel(h_ref, wg_ref, p_ref, wp_ref, x_ref, o_ref, wgb_ref, wpb_ref):
    @pl.when(pl.program_id(1) == 0)
    def _():
        wgb_ref[...] = _bf(wg_ref[...])
        wpb_ref[...] = _bf(wp_ref[...])

    gate = _sigmoid(_dot(h_ref[...], wgb_ref[...]))
    proj = _dot(_bf(p_ref[...]), wpb_ref[...])
    o_ref[...] = x_ref[...] + proj * gate


def _ple(h, wg, p, wp, x, layer, *, tm=1024, tn=512):
    m, d = h.shape
    n = wg.shape[2]
    tm = min(tm, m)
    return pl.pallas_call(
        _ple_kernel,
        grid=(n // tn, m // tm),
        in_specs=[pl.BlockSpec((tm, d), lambda j, i: (i, 0)), pl.BlockSpec((None, d, tn), lambda j, i: (layer, 0, j)),
                  pl.BlockSpec((tm, PLE_DIM), lambda j, i: (i, 0)),
                  pl.BlockSpec((None, PLE_DIM, tn), lambda j, i: (layer, 0, j)),
                  pl.BlockSpec((tm, tn), lambda j, i: (i, j))],
        out_specs=pl.BlockSpec((tm, tn), lambda j, i: (i, j)),
        out_shape=jax.ShapeDtypeStruct((m, n), jnp.float32),
        scratch_shapes=[pltpu.VMEM((d, tn), jnp.bfloat16), pltpu.VMEM((PLE_DIM, tn), jnp.bfloat16)],
        compiler_params=_cparams(("parallel", "arbitrary")),
        name="ple",
    )(h, wg, p, wp, x)


def _ffn_act_kernel(g_ref, u_ref, buf_ref, w_ref, b_ref, act_ref, nb_ref, *, nseq, t, rc):
    w = w_ref[...]
    bias = b_ref[...]
    for s in range(nseq):
        buf = buf_ref[s]

        def body(ci, carry, s=s, buf=buf):
            r0 = pl.multiple_of(ci * rc, rc)
            cur = g_ref[s, pl.ds(r0, rc), :]
            p0 = pl.multiple_of(jnp.maximum(r0 - SUBLANES, 0), SUBLANES)
            prev = g_ref[s, pl.ds(p0, SUBLANES), :]
            first = ci == 0
            hm1 = jnp.where(first, buf[1:2], prev[SUBLANES - 1:SUBLANES])
            hm2 = jnp.where(first, buf[0:1], prev[SUBLANES - 2:SUBLANES - 1])
            row = lax.broadcasted_iota(jnp.int32, cur.shape, 0)
            g1 = jnp.where(row == 0, hm1, pltpu.roll(cur, 1, axis=0))
            g2 = jnp.where(row == 0, hm2, jnp.where(row == 1, hm1, pltpu.roll(cur, 2, axis=0)))
            c = w[0:1] * g2 + w[1:2] * g1 + w[2:3] * cur + bias
            act_ref[s, pl.ds(r0, rc), :] = (c * _sigmoid(c) * u_ref[s, pl.ds(r0, rc), :]).astype(act_ref.dtype)
            return carry

        lax.fori_loop(0, t // rc, body, 0)
        nb_ref[s] = g_ref[s, t - 2:t, :]


def _ffn_act(gu, buf, w, b, out_dtype, *, nseq, tc):
    bsz, t, _ = gu.shape
    nc = D_FF // tc
    rc = min(t, ROW_CHUNK)
    return pl.pallas_call(
        functools.partial(_ffn_act_kernel, nseq=nseq, t=t, rc=rc),
        grid=(bsz // nseq, nc),
        in_specs=[pl.BlockSpec((nseq, t, tc), lambda bi, j: (bi, 0, j)),
                  pl.BlockSpec((nseq, t, tc), lambda bi, j: (bi, 0, j + nc)),
                  pl.BlockSpec((nseq, FFN_CONV - 1, tc), lambda bi, j: (bi, 0, j)),
                  pl.BlockSpec((FFN_CONV, tc), lambda bi, j: (0, j)),
                  pl.BlockSpec((1, tc), lambda bi, j: (0, j))],
        out_specs=[pl.BlockSpec((nseq, t, tc), lambda bi, j: (bi, 0, j)),
                   pl.BlockSpec((nseq, FFN_CONV - 1, tc), lambda bi, j: (bi, 0, j))],
        out_shape=[jax.ShapeDtypeStruct((bsz, t, D_FF), out_dtype),
                   jax.ShapeDtypeStruct((bsz, FFN_CONV - 1, D_FF), jnp.float32)],
        compiler_params=_cparams(("parallel", "parallel")),
        name="ffn_act",
    )(gu, gu, buf, w, b)


def _conformer_kernel(u_ref, buf_ref, w_ref, b_ref, lg_ref, lb_ref, o_ref, nb_ref, ext_ref, *, tt, rc):
    ti = pl.program_id(1)

    @pl.when(ti == 0)
    def _():
        ext_ref[0:CONV_A_HALO, :] = buf_ref[0]

    x = u_ref[0]
    ext_ref[CONV_A_HALO:CONV_A_HALO + tt, :] = x[:, :CONV_A_DIM] * _sigmoid(x[:, CONV_A_DIM:])
    bias = b_ref[...]
    lg = lg_ref[...]
    lb = lb_ref[...]
    first_tap = CONV_A_HALO - (CONV_A_WIDTH - 1)
    for r in range(tt // rc):
        r0 = r * rc
        acc = jnp.zeros((rc, CONV_A_DIM), jnp.float32) + bias
        for k in range(CONV_A_WIDTH):
            acc = acc + w_ref[k:k + 1, :] * ext_ref[r0 + first_tap + k:r0 + first_tap + k + rc, :]
        mu = jnp.mean(acc, axis=-1, keepdims=True)
        d = acc - mu
        var = jnp.mean(d * d, axis=-1, keepdims=True)
        y = d * lax.rsqrt(var + EPS) * lg + lb
        o_ref[0, r0:r0 + rc, :] = (y * _sigmoid(y)).astype(o_ref.dtype)
    carry = ext_ref[tt:tt + CONV_A_HALO, :]
    nb_ref[0] = carry
    ext_ref[0:CONV_A_HALO, :] = carry


def _conformer(u3, buf, w, b, lg, lb, out_dtype):
    bsz, t, _ = u3.shape
    tt = min(t, ROW_CHUNK)
    rc = min(tt, CONV_A_ROWS)
    vec = lambda: pl.BlockSpec((1, CONV_A_DIM), lambda bi, ti: (0, 0))
    return pl.pallas_call(
        functools.partial(_conformer_kernel, tt=tt, rc=rc),
        grid=(bsz, t // tt),
        in_specs=[pl.BlockSpec((1, tt, 2 * CONV_A_DIM), lambda bi, ti: (bi, ti, OFF_CONV // (2 * CONV_A_DIM))),
                  pl.BlockSpec((1, CONV_A_HALO, CONV_A_DIM), lambda bi, ti: (bi, 0, 0)),
                  pl.BlockSpec((CONV_A_WIDTH, CONV_A_DIM), lambda bi, ti: (0, 0)),
                  vec(), vec(), vec()],
        out_specs=[pl.BlockSpec((1, tt, CONV_A_DIM), lambda bi, ti: (bi, ti, 0)),
                   pl.BlockSpec((1, CONV_A_HALO, CONV_A_DIM), lambda bi, ti: (bi, 0, 0))],
        out_shape=[jax.ShapeDtypeStruct((bsz, t, CONV_A_DIM), out_dtype),
                   jax.ShapeDtypeStruct((bsz, CONV_A_HALO, CONV_A_DIM), jnp.float32)],
        scratch_shapes=[pltpu.VMEM((CONV_A_HALO + tt, CONV_A_DIM), jnp.float32)],
        compiler_params=_cparams(("parallel", "arbitrary")),
        name="conformer",
    )(u3, buf, w, b.reshape(1, -1), lg.reshape(1, -1), lb.reshape(1, -1))


def _ssd_kernel(z_ref, xbc_lo_ref, xbc_hi_ref, dt_ref, cbuf_ref, h0_ref, cw_ref, cb_ref, dtb_ref, alog_ref, dskip_ref,
                ng_ref, y_ref, hout_ref, cbout_ref, ext_ref, h_ref, ys_ref, *, valid):
    f32 = jnp.float32
    ln = SSM_CHUNK
    ci = pl.program_id(1)

    @pl.when(ci == 0)
    def _():
        ext_ref[0:SSM_HALO, :] = cbuf_ref[0]
        h_ref[...] = h0_ref[0]

    ext_ref[SSM_HALO:SSM_HALO + ln, 0:SSM_CONV_DIM // 2] = xbc_lo_ref[0]
    ext_ref[SSM_HALO:SSM_HALO + ln, SSM_CONV_DIM // 2:SSM_CONV_DIM] = xbc_hi_ref[0]
    conv = jnp.zeros((ln, SSM_CONV_DIM), f32) + cb_ref[...]
    for k in range(SSM_CONV):
        s0 = SSM_HALO - (SSM_CONV - 1) + k
        conv = conv + cw_ref[k:k + 1, :] * ext_ref[s0:s0 + ln, :]
    c = conv * _sigmoid(conv)
    gn = SSM_GROUPS * SSM_STATE

    row = lax.broadcasted_iota(jnp.int32, (ln, LANES), 0)
    lane = lax.broadcasted_iota(jnp.int32, (ln, LANES), 1)
    xr = dt_ref[0] + dtb_ref[...]
    dt = jnp.maximum(xr, 0.0) + jnp.log1p(jnp.exp(-jnp.abs(xr)))
    if valid < ln:
        dt = jnp.where(row < valid, dt, 0.0)
    la = dt * (-jnp.exp(alog_ref[...]))
    tril = jnp.where(row >= lane, 1.0, 0.0).astype(f32)
    acum = _dot_f32(tril, la)
    acum_t = acum.T
    alast = acum[ln - 1:ln, :]
    causal = row >= lane
    lane_lo = lane < SSM_HEADDIM
    pair_w = 2 * SSM_HEADDIM

    cb_cache = {}
    for j in range(SSM_HEADS // 2):
        h0i, h1i = 2 * j, 2 * j + 1
        g = h0i // (SSM_HEADS // SSM_GROUPS)
        bm = _bf(c[:, SSM_DIM + g * SSM_STATE:SSM_DIM + (g + 1) * SSM_STATE])
        cm = _bf(c[:, SSM_DIM + gn + g * SSM_STATE:SSM_DIM + gn + (g + 1) * SSM_STATE])
        if g not in cb_cache:
            cb_cache[g] = _dot_nt(cm, bm)
        cbg = cb_cache[g]
        a0, a1 = acum[:, h0i:h0i + 1], acum[:, h1i:h1i + 1]
        dec0 = jnp.where(causal, jnp.exp(a0 - acum_t[h0i:h0i + 1, :]), 0.0)
        dec1 = jnp.where(causal, jnp.exp(a1 - acum_t[h1i:h1i + 1, :]), 0.0)
        sc = jnp.concatenate([_bf(cbg * dec0), _bf(cbg * dec1)], axis=1)
        xs = c[:, j * pair_w:(j + 1) * pair_w]
        xdt = xs * jnp.where(lane_lo, dt[:, h0i:h0i + 1], dt[:, h1i:h1i + 1])
        xblk = jnp.concatenate([_bf(jnp.where(lane_lo, xdt, 0.0)), _bf(jnp.where(lane_lo, 0.0, xdt))], axis=0)
        y_diag = _dot(sc, xblk)
        hp = h_ref[j * pair_w:(j + 1) * pair_w, :]
        y_off = _dot_nt(cm, _bf(hp)) * jnp.where(lane_lo, jnp.exp(a0), jnp.exp(a1))
        ys_ref[:, j * pair_w:(j + 1) * pair_w] = y_diag + y_off + xs * dskip_ref[:, j * pair_w:(j + 1) * pair_w]
        al0, al1 = alast[:, h0i:h0i + 1], alast[:, h1i:h1i + 1]
        dend = jnp.where(lane_lo, jnp.exp(al0 - a0), jnp.exp(al1 - a1))
        s_new = _dot(_bf((xdt * dend).T), bm)
        h_ref[j * pair_w:(j + 1) * pair_w, :] = jnp.where(row < SSM_HEADDIM, jnp.exp(al0), jnp.exp(al1)) * hp + s_new

    z = z_ref[0]
    y = ys_ref[...] * (z * _sigmoid(z))
    gw = SSM_DIM // SSM_GROUPS
    for g in range(SSM_GROUPS):
        yg = y[:, g * gw:(g + 1) * gw]
        ms = jnp.mean(yg * yg, axis=-1, keepdims=True)
        y_ref[0, :, g * gw:(g + 1) * gw] = (yg * lax.rsqrt(ms + EPS) * ng_ref[:, g * gw:(g + 1) * gw]).astype(y_ref.dtype)

    tail = ext_ref[valid:valid + SSM_HALO, :]
    cbout_ref[0] = tail
    ext_ref[0:SSM_HALO, :] = tail
    hout_ref[0] = h_ref[...]


def _ssd(u3, tail3, cbuf, h0, cw, cb, dtb, alog, dskip, ng, valid, out_dtype):
    bsz, t, _ = u3.shape
    ln = SSM_CHUNK
    half = SSM_CONV_DIM // 2
    full = lambda shape: pl.BlockSpec(shape, lambda bi, ci: (0,) * len(shape))
    return pl.pallas_call(
        functools.partial(_ssd_kernel, valid=valid),
        grid=(bsz, t // ln),
        in_specs=[pl.BlockSpec((1, ln, SSM_DIM), lambda bi, ci: (bi, ci, OFF_Z // SSM_DIM)),
                  pl.BlockSpec((1, ln, half), lambda bi, ci: (bi, ci, OFF_XBC // half)),
                  pl.BlockSpec((1, ln, half), lambda bi, ci: (bi, ci, OFF_XBC // half + 1)),
                  pl.BlockSpec((1, ln, U_TAIL), lambda bi, ci: (bi, ci, 0)),
                  pl.BlockSpec((1, SSM_HALO, SSM_CONV_DIM), lambda bi, ci: (bi, 0, 0)),
                  pl.BlockSpec((1, SSM_DIM, SSM_STATE), lambda bi, ci: (bi, 0, 0)),
                  full((SSM_CONV, SSM_CONV_DIM)), full((1, SSM_CONV_DIM)), full((1, LANES)), full((1, LANES)),
                  full((1, SSM_DIM)), full((1, SSM_DIM))],
        out_specs=[pl.BlockSpec((1, ln, SSM_DIM), lambda bi, ci: (bi, ci, 0)),
                   pl.BlockSpec((1, SSM_DIM, SSM_STATE), lambda bi, ci: (bi, 0, 0)),
                   pl.BlockSpec((1, SSM_HALO, SSM_CONV_DIM), lambda bi, ci: (bi, 0, 0))],
        out_shape=[jax.ShapeDtypeStruct((bsz, t, SSM_DIM), out_dtype),
                   jax.ShapeDtypeStruct((bsz, SSM_DIM, SSM_STATE), jnp.float32),
                   jax.ShapeDtypeStruct((bsz, SSM_HALO, SSM_CONV_DIM), jnp.float32)],
        scratch_shapes=[pltpu.VMEM((SSM_HALO + ln, SSM_CONV_DIM), jnp.float32),
                        pltpu.VMEM((SSM_DIM, SSM_STATE), jnp.float32),
                        pltpu.VMEM((ln, SSM_DIM), jnp.float32)],
        compiler_params=_cparams(("parallel", "arbitrary")),
        name="ssd",
    )(u3, u3, u3, tail3, cbuf, h0, cw, cb, dtb, alog, dskip, ng)


def _nsa_prep_kernel(q_ref, kvc_ref, kvs_ref, kvw_ref, cos_ref, sin_ref, qg_ref, kg_ref,
                     qn_ref, qr_ref, kc_ref, vc_ref, ks_ref, vs_ref, kw_ref, vw_ref):
    cos = cos_ref[...]
    sin = sin_ref[...]

    def norm(x, g):
        return x * lax.rsqrt(jnp.mean(x * x, axis=-1, keepdims=True) + EPS) * g

    def rope(x):
        return x * cos + pltpu.roll(x, HEAD_DIM // 2, axis=1) * sin

    qg = qg_ref[...]
    for h in range(NSA_HEADS):
        sl = slice(h * HEAD_DIM, (h + 1) * HEAD_DIM)
        x = norm(q_ref[:, sl], qg)
        qn_ref[:, sl] = x.astype(qn_ref.dtype)
        qr_ref[:, sl] = rope(x).astype(qr_ref.dtype)
    for h in range(NSA_KV_HEADS):
        sl = slice(h * HEAD_DIM, (h + 1) * HEAD_DIM)
        sv = slice(KV_DIM + h * HEAD_DIM, KV_DIM + (h + 1) * HEAD_DIM)
        kc_ref[:, sl] = norm(kvc_ref[:, sl], kg_ref[0:1, :])
        vc_ref[:, sl] = kvc_ref[:, sv]
        ks_ref[:, sl] = rope(norm(kvs_ref[:, sl], kg_ref[1:2, :]))
        vs_ref[:, sl] = kvs_ref[:, sv]
        kw_ref[:, sl] = rope(norm(kvw_ref[:, sl], kg_ref[2:3, :]))
        vw_ref[:, sl] = kvw_ref[:, sv]


def _nsa_prep(u, cos, sin, qg, kg, q_dtype):
    m = u.shape[0]
    tt = min(m, ROW_CHUNK)
    row = lambda w, blk=0: pl.BlockSpec((tt, w), lambda i: (i, blk))
    kv = jax.ShapeDtypeStruct((m, KV_DIM), jnp.float32)
    qo = jax.ShapeDtypeStruct((m, NSA_DIM), q_dtype)
    kv0 = OFF_KV // (2 * KV_DIM)
    return pl.pallas_call(
        _nsa_prep_kernel,
        grid=(m // tt,),
        in_specs=[row(NSA_DIM, OFF_Q // NSA_DIM), row(2 * KV_DIM, kv0), row(2 * KV_DIM, kv0 + 1), row(2 * KV_DIM, kv0 + 2),
                  row(HEAD_DIM), row(HEAD_DIM),
                  pl.BlockSpec((1, HEAD_DIM), lambda i: (0, 0)), pl.BlockSpec((3, HEAD_DIM), lambda i: (0, 0))],
        out_specs=[row(NSA_DIM), row(NSA_DIM)] + [row(KV_DIM)] * 6,
        out_shape=[qo, qo] + [kv] * 6,
        compiler_params=_cparams(("parallel",)),
        name="nsa_prep",
    )(u, u, u, u, cos, sin, qg.reshape(1, HEAD_DIM), kg)


def _softmax_rows(s):
    m = jnp.max(s, axis=-1, keepdims=True)
    e = jnp.exp(s - m)
    return e / jnp.sum(e, axis=-1, keepdims=True)


def _nsa_attn_kernel(qn_ref, qr_ref, tail_ref, kc_ref, vc_ref, ks_ref, vs_ref, kw_ref, vw_ref, pe_ref, cw_ref,
                     o_ref, kcmp_ref, kcmpp_ref, vcmp_ref, ksb_ref, vsb_ref, kwb_ref, vwb_ref,
                     m_ref, l_ref, acc_ref, *, t, tq):
    f32 = jnp.float32
    kvh = pl.program_id(1)
    qi = pl.program_id(2)
    ncb = t // CMP_BLOCK
    nsb = t // SEL_BLOCK
    rows = GQA * tq
    span = min(WINDOW + tq, t)
    cmp_shift = CMP_BLOCK.bit_length() - 1
    sel_shift = SEL_BLOCK.bit_length() - 1

    @pl.when(qi == 0)
    def _():
        r = lax.broadcasted_iota(jnp.int32, (ncb, t), 0)
        cblk = lax.broadcasted_iota(jnp.int32, (ncb, t), 1) >> cmp_shift
        avg_nat = jnp.where(cblk == r, 1.0 / CMP_BLOCK, 0.0).astype(f32)
        perm = jnp.where(r < nsb, 2 * r, 2 * (r - nsb) + 1)
        avg_perm = jnp.where(cblk == perm, 1.0 / CMP_BLOCK, 0.0).astype(f32)
        kc = kc_ref[...]
        pe_k = jnp.mean(pe_ref[0], axis=0, keepdims=True)
        pe_v = jnp.mean(pe_ref[1], axis=0, keepdims=True)
        wk = _bf(cw_ref[0])
        wv = _bf(cw_ref[1])
        kcmp_ref[...] = _bf(_dot(_bf(_dot_f32(avg_nat, kc) + pe_k), wk))
        kcmpp_ref[...] = _bf(_dot(_bf(_dot_f32(avg_perm, kc) + pe_k), wk))
        vcmp_ref[...] = _bf(_dot(_bf(_dot_f32(avg_nat, vc_ref[...]) + pe_v), wv))
        ksb_ref[...] = _bf(ks_ref[...])
        vsb_ref[...] = _bf(vs_ref[...])
        kwb_ref[...] = _bf(kw_ref[...])
        vwb_ref[...] = _bf(vw_ref[...])

    t0 = qi * tq

    def stack(ref):
        return jnp.concatenate([ref[:, g * HEAD_DIM:(g + 1) * HEAD_DIM] for g in range(GQA)], axis=0)

    qn = stack(qn_ref)

    tpos_c = t0 + (lax.broadcasted_iota(jnp.int32, (rows, ncb), 0) & (tq - 1))
    blk_end = (lax.broadcasted_iota(jnp.int32, (rows, ncb), 1) + 1) * CMP_BLOCK - 1
    s = jnp.where(blk_end <= tpos_c, _dot_nt(qn, kcmp_ref[...]) * ATTN_SCALE, NEG)
    anyvis = jnp.where(tpos_c[:, 0:1] >= CMP_BLOCK - 1, 1.0, 0.0).astype(f32)
    p = _softmax_rows(s) * anyvis
    o_cmp = _dot(_bf(p), vcmp_ref[...])

    rperm = lax.broadcasted_iota(jnp.int32, (ncb, rows), 0)
    blk_t = jnp.where(rperm < nsb, 2 * rperm, 2 * (rperm - nsb) + 1)
    tpos_t = t0 + (lax.broadcasted_iota(jnp.int32, (ncb, rows), 1) & (tq - 1))
    st = jnp.where((blk_t + 1) * CMP_BLOCK - 1 <= tpos_t, _dot_nt(kcmpp_ref[...], qn) * ATTN_SCALE, NEG)
    mt = jnp.max(st, axis=0, keepdims=True)
    et = jnp.exp(st - mt)
    pt = et / jnp.sum(et, axis=0, keepdims=True) * jnp.where(tpos_t[0:1, :] >= CMP_BLOCK - 1, 1.0, 0.0).astype(f32)
    psum = pt[:, 0:tq]
    for g in range(1, GQA):
        psum = psum + pt[:, g * tq:(g + 1) * tq]
    imp = psum[0:nsb, :] + psum[nsb:2 * nsb, :]
    jrow = lax.broadcasted_iota(jnp.int32, (nsb, tq), 0)
    qp = t0 + lax.broadcasted_iota(jnp.int32, (nsb, tq), 1)
    forced = (jrow == (qp >> sel_shift)) | (jrow == 0)
    imp = jnp.where(forced, FORCE, jnp.where(jrow * SEL_BLOCK > qp, NEG, imp))
    cnt = jnp.zeros((nsb, tq), f32)
    for i in range(nsb):
        ri = imp[i:i + 1, :]
        cnt = cnt + jnp.where((ri > imp) | ((ri == imp) & (jrow > i)), 1.0, 0.0)
    sel_t = jnp.where(cnt < SEL_TOPK, 1.0, 0.0).astype(f32)
    sel_pad = jnp.concatenate([sel_t, jnp.zeros((LANES - nsb, tq), f32)], axis=0) if nsb < LANES else sel_t
    sel_q = _bf(sel_pad.T)

    kc_sz = min(NSA_KC, t)
    m_ref[...] = jnp.full((rows, 1), NEG, f32)
    l_ref[...] = jnp.zeros((rows, 1), f32)
    acc_ref[...] = jnp.zeros((rows, HEAD_DIM), f32)
    qpos_s = t0 + lax.broadcasted_iota(jnp.int32, (tq, kc_sz), 0)

    def sel_body(ck, carry):
        k0 = pl.multiple_of(ck * kc_sz, kc_sz)
        kb = ksb_ref[pl.ds(k0, kc_sz), :]
        vb = vsb_ref[pl.ds(k0, kc_sz), :]
        kpos = k0 + lax.broadcasted_iota(jnp.int32, (tq, kc_sz), 1)
        eblk = (k0 + lax.broadcasted_iota(jnp.int32, (LANES, kc_sz), 1)) >> sel_shift
        expand = jnp.where(eblk == lax.broadcasted_iota(jnp.int32, (LANES, kc_sz), 0), 1.0, 0.0).astype(jnp.bfloat16)
        chosen = _dot(sel_q, expand)
        visible = (chosen > 0.5) & (kpos <= qpos_s)
        for g in range(GQA):
            rs = slice(g * tq, (g + 1) * tq)
            sc = jnp.where(visible, _dot_nt(qr_ref[:, g * HEAD_DIM:(g + 1) * HEAD_DIM], kb) * ATTN_SCALE, NEG)
            m_old = m_ref[rs, :]
            m_new = jnp.maximum(m_old, jnp.max(sc, axis=-1, keepdims=True))
            alpha = jnp.exp(m_old - m_new)
            pc = jnp.exp(sc - m_new)
            l_ref[rs, :] = alpha * l_ref[rs, :] + jnp.sum(pc, axis=-1, keepdims=True)
            acc_ref[rs, :] = alpha * acc_ref[rs, :] + _dot(_bf(pc), vb)
            m_ref[rs, :] = m_new
        return carry

    lax.fori_loop(0, (t0 + tq + kc_sz - 1) // kc_sz, sel_body, 0)

    ws = pl.multiple_of(jnp.maximum(t0 + tq - span, 0), SWA_QBLOCK)
    kb = kwb_ref[pl.ds(ws, span), :]
    vb = vwb_ref[pl.ds(ws, span), :]
    diff = (t0 + lax.broadcasted_iota(jnp.int32, (tq, span), 0)) - (ws + lax.broadcasted_iota(jnp.int32, (tq, span), 1))
    win_vis = (diff >= 0) & (diff <= WINDOW)
    sg = _sigmoid(tail_ref[...])
    glane = lax.broadcasted_iota(jnp.int32, (tq, U_TAIL), 1)
    for g in range(GQA):
        rs = slice(g * tq, (g + 1) * tq)
        hs = slice(g * HEAD_DIM, (g + 1) * HEAD_DIM)
        sw = jnp.where(win_vis, _dot_nt(qr_ref[:, hs], kb) * ATTN_SCALE, NEG)
        o_swa = _dot(_bf(_softmax_rows(sw)), vb)
        o_sel = acc_ref[rs, :] / l_ref[rs, :]
        lane0 = TAIL_GATES + (kvh * GQA + g) * 3
        gate = lambda br: jnp.sum(jnp.where(glane == lane0 + br, sg, 0.0), axis=-1, keepdims=True)
        o_ref[:, hs] = (gate(0) * o_cmp[rs] + gate(1) * o_sel + gate(2) * o_swa).astype(o_ref.dtype)


def _nsa_attn_prompt(qn, qr, tail, kc, vc, ks, vs, kw, vw, pe, cw, bsz, t):
    tq = NSA_TQ
    nq = t // tq
    ncb = t // CMP_BLOCK
    rows = GQA * tq
    qspec = pl.BlockSpec((tq, GQA * HEAD_DIM), lambda b, k, q: (b * nq + q, k))
    kvspec = pl.BlockSpec((t, HEAD_DIM), lambda b, k, q: (b, k))
    bf16 = jnp.bfloat16
    return pl.pallas_call(
        functools.partial(_nsa_attn_kernel, t=t, tq=tq),
        grid=(bsz, NSA_KV_HEADS, nq),
        in_specs=[qspec, qspec, pl.BlockSpec((tq, U_TAIL), lambda b, k, q: (b * nq + q, 0))] + [kvspec] * 6 + [
            pl.BlockSpec((2, CMP_BLOCK, HEAD_DIM), lambda b, k, q: (0, 0, 0)),
            pl.BlockSpec((2, HEAD_DIM, HEAD_DIM), lambda b, k, q: (0, 0, 0))],
        out_specs=qspec,
        out_shape=jax.ShapeDtypeStruct((bsz * t, NSA_DIM), bf16),
        scratch_shapes=[pltpu.VMEM((ncb, HEAD_DIM), bf16), pltpu.VMEM((ncb, HEAD_DIM), bf16),
                        pltpu.VMEM((ncb, HEAD_DIM), bf16)] + [pltpu.VMEM((t, HEAD_DIM), bf16)] * 4 + [
                        pltpu.VMEM((rows, 1), jnp.float32), pltpu.VMEM((rows, 1), jnp.float32),
                        pltpu.VMEM((rows, HEAD_DIM), jnp.float32)],
        compiler_params=_cparams(("parallel", "parallel", "arbitrary")),
        name="nsa_attn",
    )(qn, qr, tail, kc, vc, ks, vs, kw, vw, pe, cw)


NSA_PP = 8


def _nsa_sample_kernel(pt_ref, qn_ref, qr_ref, tail_ref, kcn_ref, vcn_ref, ksn_ref, vsn_ref, kwn_ref, vwn_ref,
                       swk_ref, swv_ref, pe_ref, cw_ref, *rest, tq, past_len, ns, ne):
    pp = NSA_PP
    ck_refs, cv_refs = rest[0:pp], rest[pp:2 * pp]
    sk_refs, sv_refs = rest[2 * pp:3 * pp], rest[3 * pp:4 * pp]
    (o_ref, swk_out_ref, swv_out_ref,
     ke_ref, ko_ref, ve_ref, vo_ref, sel_ref, ocmp_ref, m_ref, l_ref, acc_ref) = rest[4 * pp:]
    f32 = jnp.float32
    s = pl.program_id(1)
    rows = GQA * tq
    blk_per_page = PAGE_SIZE // CMP_BLOCK
    half = pp * blk_per_page // 2
    nsb = past_len // SEL_BLOCK + 1
    cmp_shift = CMP_BLOCK.bit_length() - 1
    sel_shift = SEL_BLOCK.bit_length() - 1
    hd = lambda k: slice(k * HEAD_DIM, (k + 1) * HEAD_DIM)

    def pe_mean(i):
        v = jnp.mean(pe_ref[i], axis=0, keepdims=True)
        return jnp.concatenate([v] * NSA_KV_HEADS, axis=1)

    def stack_q(ref, k):
        return _bf(jnp.concatenate([ref[:, hd(k * GQA + g)] for g in range(GQA)], axis=0))

    def tile_rows(x):
        return jnp.concatenate([x] * GQA, axis=0)

    @pl.when(s < ns)
    def _():
        r32 = lax.broadcasted_iota(jnp.int32, (2 * half, PAGE_SIZE), 0)
        cblk = lax.broadcasted_iota(jnp.int32, (2 * half, PAGE_SIZE), 1) >> cmp_shift
        acc_k = jnp.zeros((2 * half, KV_DIM), f32)
        acc_v = jnp.zeros((2 * half, KV_DIM), f32)
        for r in range(pp):
            tgt = jnp.where((cblk & 1) == 0, 0, half) + (blk_per_page // 2) * r + (cblk >> 1)
            avg = jnp.where(r32 == tgt, 1.0 / CMP_BLOCK, 0.0).astype(f32)
            acc_k = acc_k + _dot_f32(avg, ck_refs[r][0])
            acc_v = acc_v + _dot_f32(avg, cv_refs[r][0])
        mk = acc_k + pe_mean(0)
        mv = acc_v + pe_mean(1)
        wk = _bf(cw_ref[0])
        wv = _bf(cw_ref[1])
        r0 = pl.multiple_of(s * half, half)
        for k in range(NSA_KV_HEADS):
            kp = _bf(_dot(_bf(mk[:, hd(k)]), wk))
            vp = _bf(_dot(_bf(mv[:, hd(k)]), wv))
            ke_ref[pl.ds(r0, half), hd(k)] = kp[0:half]
            ko_ref[pl.ds(r0, half), hd(k)] = kp[half:2 * half]
            ve_ref[pl.ds(r0, half), hd(k)] = vp[0:half]
            vo_ref[pl.ds(r0, half), hd(k)] = vp[half:2 * half]

    @pl.when(s == ns - 1)
    def _():
        n_past = past_len // (2 * CMP_BLOCK)
        tail_rows = ne - n_past
        trow = lax.broadcasted_iota(jnp.int32, (tail_rows, KV_DIM), 0)
        wk = _bf(cw_ref[0])
        wv = _bf(cw_ref[1])

        def new_blocks(new_ref, pe_i, w):
            m_even = jnp.sum(new_ref[...], axis=0, keepdims=True) * (1.0 / CMP_BLOCK) + pe_mean(pe_i)
            m_odd = pe_mean(pe_i)
            me = jnp.where(trow == 0, m_even, 0.0)
            mo = jnp.where(trow == 0, m_odd, 0.0)
            pe_cols = [_bf(_dot(_bf(me[:, hd(k)]), w)) for k in range(NSA_KV_HEADS)]
            po_cols = [_bf(_dot(_bf(mo[:, hd(k)]), w)) for k in range(NSA_KV_HEADS)]
            return jnp.concatenate(pe_cols, axis=1), jnp.concatenate(po_cols, axis=1)

        kne, kno = new_blocks(kcn_ref, 0, wk)
        vne, vno = new_blocks(vcn_ref, 1, wv)
        ke_ref[n_past:ne, :] = kne
        ko_ref[n_past:ne, :] = kno
        ve_ref[n_past:ne, :] = vne
        vo_ref[n_past:ne, :] = vno

        idx = lax.broadcasted_iota(jnp.int32, (rows, ne), 1)
        qpos = past_len + (lax.broadcasted_iota(jnp.int32, (rows, ne), 0) & (tq - 1))
        real = idx <= n_past
        vis_e = real & ((2 * idx + 1) * CMP_BLOCK - 1 <= qpos)
        vis_o = real & ((2 * idx + 2) * CMP_BLOCK - 1 <= qpos)
        anyvis = jnp.where(qpos[:, 0:1] >= CMP_BLOCK - 1, 1.0, 0.0).astype(f32)
        imps = []
        for k in range(NSA_KV_HEADS):
            q = stack_q(qn_ref, k)
            se = jnp.where(vis_e, _dot_nt(q, ke_ref[:, hd(k)]) * ATTN_SCALE, NEG)
            so = jnp.where(vis_o, _dot_nt(q, ko_ref[:, hd(k)]) * ATTN_SCALE, NEG)
            mx = jnp.maximum(jnp.max(se, axis=-1, keepdims=True), jnp.max(so, axis=-1, keepdims=True))
            ee = jnp.exp(se - mx)
            eo = jnp.exp(so - mx)
            den = jnp.sum(ee, axis=-1, keepdims=True) + jnp.sum(eo, axis=-1, keepdims=True)
            p_e = ee / den * anyvis
            p_o = eo / den * anyvis
            ocmp_ref[k * rows:(k + 1) * rows, :] = _dot(_bf(p_e), ve_ref[:, hd(k)]) + _dot(_bf(p_o), vo_ref[:, hd(k)])
            se_sum = p_e[0:tq]
            so_sum = p_o[0:tq]
            for g in range(1, GQA):
                se_sum = se_sum + p_e[g * tq:(g + 1) * tq]
                so_sum = so_sum + p_o[g * tq:(g + 1) * tq]
            imps.append(se_sum + so_sum)
        imp = jnp.concatenate(imps, axis=0)
        nrow = NSA_KV_HEADS * tq
        j = lax.broadcasted_iota(jnp.int32, (nrow, ne), 1)
        qp = past_len + (lax.broadcasted_iota(jnp.int32, (nrow, ne), 0) & (tq - 1))
        forced = (j == (qp >> sel_shift)) | (j == 0)
        imp = jnp.where(forced, FORCE, jnp.where(j * SEL_BLOCK > qp, NEG, imp))
        imp = jnp.where(j < nsb, imp, 2.0 * NEG)
        cnt = jnp.zeros((nrow, ne), f32)
        for i in range(nsb):
            ci = imp[:, i:i + 1]
            cnt = cnt + jnp.where((ci > imp) | ((ci == imp) & (j > i)), 1.0, 0.0)
        sel_ref[...] = jnp.where((cnt < min(SEL_TOPK, nsb)) & (j < nsb), 1.0, 0.0).astype(f32)
        m_ref[...] = jnp.full(m_ref.shape, NEG, f32)
        l_ref[...] = jnp.zeros(l_ref.shape, f32)
        acc_ref[...] = jnp.zeros(acc_ref.shape, f32)

    def online_update(k, sc, vb):
        rs = slice(k * rows, (k + 1) * rows)
        m_old = m_ref[rs, :]
        m_new = jnp.maximum(m_old, jnp.max(sc, axis=-1, keepdims=True))
        alpha = jnp.exp(m_old - m_new)
        pc = jnp.exp(sc - m_new)
        l_ref[rs, :] = alpha * l_ref[rs, :] + jnp.sum(pc, axis=-1, keepdims=True)
        acc_ref[rs, :] = alpha * acc_ref[rs, :] + _dot(_bf(pc), vb)
        m_ref[rs, :] = m_new

    @pl.when(s >= ns)
    def _():
        s2 = s - ns
        nkeys = pp * PAGE_SIZE
        kcat = jnp.concatenate([sk_refs[r][0] for r in range(pp)], axis=0)
        vcat = jnp.concatenate([sv_refs[r][0] for r in range(pp)], axis=0)
        jrow = lax.broadcasted_iota(jnp.int32, (ne, nkeys), 0)
        kblk = s2 * (nkeys // SEL_BLOCK) + (lax.broadcasted_iota(jnp.int32, (ne, nkeys), 1) >> sel_shift)
        expand = jnp.where(jrow == kblk, 1.0, 0.0).astype(jnp.bfloat16)
        chosen = _dot(_bf(sel_ref[...]), expand)
        kpos = s2 * nkeys + lax.broadcasted_iota(jnp.int32, (tq, nkeys), 1)
        qpos = past_len + lax.broadcasted_iota(jnp.int32, (tq, nkeys), 0)
        for k in range(NSA_KV_HEADS):
            hidden = tile_rows(jnp.where((chosen[k * tq:(k + 1) * tq] > 0.5) & (kpos <= qpos), 0.0, 1.0))
            sc = jnp.where(hidden > 0.5, NEG, _dot_nt(stack_q(qr_ref, k), _bf(kcat[:, hd(k)])) * ATTN_SCALE)
            online_update(k, sc, _bf(vcat[:, hd(k)]))

    @pl.when(s == 2 * ns - 1)
    def _():
        npad = 2 * SUBLANES
        zpad = jnp.zeros((npad - tq, KV_DIM), f32)
        pad_new = lambda ref: jnp.concatenate([ref[...], zpad], axis=0)
        ksn, vsn, kwn, vwn = pad_new(ksn_ref), pad_new(vsn_ref), pad_new(kwn_ref), pad_new(vwn_ref)
        win = swk_ref.shape[1]
        trow = lax.broadcasted_iota(jnp.int32, (rows, npad), 0) & (tq - 1)
        ncol = lax.broadcasted_iota(jnp.int32, (rows, npad), 1)
        new_vis = (ncol <= trow) & (ncol < tq)
        wdiff = (past_len + (lax.broadcasted_iota(jnp.int32, (rows, win), 0) & (tq - 1))) - (
            past_len - win + lax.broadcasted_iota(jnp.int32, (rows, win), 1))
        win_vis = (wdiff >= 0) & (wdiff <= WINDOW)
        sg = _sigmoid(tail_ref[...])
        for k in range(NSA_KV_HEADS):
            qr = stack_q(qr_ref, k)
            picked = tile_rows(sel_ref[k * tq:(k + 1) * tq, nsb - 1:nsb]) > 0.5
            sc = jnp.where(picked & new_vis, _dot_nt(qr, _bf(ksn[:, hd(k)])) * ATTN_SCALE, NEG)
            online_update(k, sc, _bf(vsn[:, hd(k)]))
            rs = slice(k * rows, (k + 1) * rows)
            o_sel = acc_ref[rs, :] / l_ref[rs, :]
            s1 = jnp.where(win_vis, _dot_nt(qr, _bf(swk_ref[0, :, hd(k)])) * ATTN_SCALE, NEG)
            s2n = jnp.where(new_vis, _dot_nt(qr, _bf(kwn[:, hd(k)])) * ATTN_SCALE, NEG)
            mx = jnp.maximum(jnp.max(s1, axis=-1, keepdims=True), jnp.max(s2n, axis=-1, keepdims=True))
            e1 = jnp.exp(s1 - mx)
            e2 = jnp.exp(s2n - mx)
            den = jnp.sum(e1, axis=-1, keepdims=True) + jnp.sum(e2, axis=-1, keepdims=True)
            o_swa = _dot(_bf(e1 / den), _bf(swv_ref[0, :, hd(k)])) + _dot(_bf(e2 / den), _bf(vwn[:, hd(k)]))
            o_cmp = ocmp_ref[rs, :]
            for g in range(GQA):
                c0 = TAIL_GATES + (k * GQA + g) * 3
                gs = slice(g * tq, (g + 1) * tq)
                o = (sg[:, c0:c0 + 1] * o_cmp[gs] + sg[:, c0 + 1:c0 + 2] * o_sel[gs] + sg[:, c0 + 2:c0 + 3] * o_swa[gs])
                o_ref[:, hd(k * GQA + g)] = o
        swk_out_ref[0, 0:win - tq, :] = swk_ref[0, tq:win, :]
        swk_out_ref[0, win - tq:win, :] = kwn_ref[...]
        swv_out_ref[0, 0:win - tq, :] = swv_ref[0, tq:win, :]
        swv_out_ref[0, win - tq:win, :] = vwn_ref[...]


def _nsa_sample(qn, qr, tail, kc, vc, ks, vs, kw, vw, swa_k, swa_v, cache_ck, cache_cv, cache_sk, cache_sv,
                page_table, pe, cw, bsz, tq, page0):
    n_pages = page_table.shape[1]
    past_len = n_pages * PAGE_SIZE
    pp = NSA_PP
    ns = n_pages // pp
    n_past = past_len // (2 * CMP_BLOCK)
    ne = -(-(n_past + 1) // LANES) * LANES
    win = swa_k.shape[1]
    rows = GQA * tq
    row = lambda w: pl.BlockSpec((tq, w), lambda b, s, pt: (b, 0))
    state = pl.BlockSpec((1, win, KV_DIM), lambda b, s, pt: (b, 0, 0))
    full3 = lambda shp: pl.BlockSpec(shp, lambda b, s, pt: (0, 0, 0))

    def page(r, phase):
        if phase == 0:
            return pl.BlockSpec((1, PAGE_SIZE, KV_DIM),
                                lambda b, s, pt: (page0 + pt[b, jnp.minimum(s, ns - 1) * pp + r], 0, 0))
        return pl.BlockSpec((1, PAGE_SIZE, KV_DIM),
                            lambda b, s, pt: (page0 + pt[b, jnp.maximum(s - ns, 0) * pp + r], 0, 0))

    in_specs = ([row(NSA_DIM), row(NSA_DIM), row(U_TAIL)] + [row(KV_DIM)] * 6 + [state, state,
                full3((2, CMP_BLOCK, HEAD_DIM)), full3((2, HEAD_DIM, HEAD_DIM))]
                + [page(r, 0) for r in range(pp)] * 2 + [page(r, 1) for r in range(pp)] * 2)
    bf16, f32 = jnp.bfloat16, jnp.float32
    grid_spec = pltpu.PrefetchScalarGridSpec(
        num_scalar_prefetch=1,
        grid=(bsz, 2 * ns),
        in_specs=in_specs,
        out_specs=[row(NSA_DIM), state, state],
        scratch_shapes=[pltpu.VMEM((ne, KV_DIM), bf16)] * 4 + [
            pltpu.VMEM((NSA_KV_HEADS * tq, ne), f32), pltpu.VMEM((NSA_KV_HEADS * rows, HEAD_DIM), f32),
            pltpu.VMEM((NSA_KV_HEADS * rows, 1), f32), pltpu.VMEM((NSA_KV_HEADS * rows, 1), f32),
            pltpu.VMEM((NSA_KV_HEADS * rows, HEAD_DIM), f32)])
    return pl.pallas_call(
        functools.partial(_nsa_sample_kernel, tq=tq, past_len=past_len, ns=ns, ne=ne),
        grid_spec=grid_spec,
        out_shape=[jax.ShapeDtypeStruct((bsz * tq, NSA_DIM), f32),
                   jax.ShapeDtypeStruct((bsz, win, KV_DIM), f32), jax.ShapeDtypeStruct((bsz, win, KV_DIM), f32)],
        compiler_params=_cparams(("parallel", "arbitrary")),
        name="nsa_sample",
    )(page_table, qn, qr, tail, kc, vc, ks, vs, kw, vw, swa_k, swa_v, pe, cw,
      *([cache_ck] * pp + [cache_cv] * pp + [cache_sk] * pp + [cache_sv] * pp))


def _rope_tables(pos, bsz):
    half = HEAD_DIM // 2
    inv = ROPE_THETA ** (-jnp.arange(half, dtype=jnp.float32) / half)
    ang = pos.astype(jnp.float32)[:, None] * inv[None, :]
    cos = jnp.cos(ang)
    sin = jnp.sin(ang)
    cos2 = jnp.concatenate([cos, cos], axis=-1)
    sin2 = jnp.concatenate([-sin, sin], axis=-1)
    return jnp.tile(cos2, (bsz, 1)), jnp.tile(sin2, (bsz, 1))


def _trunk_layer(x, p_emb, pos, lp, st, win_buf, layer):
    bsz, t, _ = x.shape
    m = bsz * t
    bf16, f32 = jnp.bfloat16, jnp.float32
    prompt = st is None
    act_dtype = bf16 if prompt else f32
    x2 = x.reshape(m, D_MODEL)
    h = _rmsnorm_bf16(x2, lp['attn_norm_g'])
    u_a = _matmul_w(h, lp['w_in'], layer, n_cols=U_A)
    u_b = _matmul(h, lp['w_in_b'])
    tail = _matmul(h, lp['w_in_tail'], tn=U_TAIL)
    u3 = u_a.reshape(bsz, t, U_A)
    tail3 = tail.reshape(bsz, t, U_TAIL)
    if prompt:
        buf_a = jnp.zeros((bsz, CONV_A_HALO, CONV_A_DIM), f32)
        buf_ssm = jnp.zeros((bsz, SSM_HALO, SSM_CONV_DIM), f32)
        h0 = jnp.zeros((bsz, SSM_DIM, SSM_STATE), f32)
        buf_f = jnp.zeros((bsz, FFN_CONV - 1, D_FF), f32)
    else:
        buf_a = jnp.pad(st['conv_a'], ((0, 0), (CONV_A_HALO - (CONV_A_WIDTH - 1), 0), (0, 0)))
        buf_ssm = jnp.pad(st['ssm_conv'], ((0, 0), (SSM_HALO - (SSM_CONV - 1), 0), (0, 0)))
        h0 = st['ssm'].reshape(bsz, SSM_DIM, SSM_STATE)
        buf_f = st['ffn_conv']

    a_out, nb_a = _conformer(u3, buf_a, lp['conv_a_w'], lp['conv_a_b'], lp['conv_a_ln_g'], lp['conv_a_ln_b'], act_dtype)
    new_a = nb_a[:, CONV_A_HALO - (CONV_A_WIDTH - 1):]

    if t % SSM_CHUNK:
        padt = ((0, 0), (0, SSM_CHUNK - t), (0, 0))
        u3s, tail3s, valid = jnp.pad(u3, padt), jnp.pad(tail3, padt), t
    else:
        u3s, tail3s, valid = u3, tail3, SSM_CHUNK
    pad_h = lambda v: jnp.pad(v.reshape(1, SSM_HEADS), ((0, 0), (0, LANES - SSM_HEADS)))
    b_out, h_new, nb_s = _ssd(u3s, tail3s, buf_ssm, h0, lp['ssm_conv_w'], lp['ssm_conv_b'].reshape(1, -1),
                              pad_h(lp['ssm_dt_bias']), pad_h(lp['ssm_a_log']),
                              jnp.repeat(lp['ssm_d'], SSM_HEADDIM).reshape(1, SSM_DIM),
                              lp['ssm_norm_g'].reshape(1, SSM_DIM), valid, bf16 if prompt else f32)
    b_out = b_out[:, :t]
    new_h = h_new.reshape(bsz, SSM_HEADS, SSM_HEADDIM, SSM_STATE)
    new_ssm_conv = nb_s[:, SSM_HALO - (SSM_CONV - 1):]

    cos, sin = _rope_tables(pos, bsz)
    qn, qr, kc, vc, ks, vs, kw, vw = _nsa_prep(u_b, cos, sin, lp['nsa_q_norm_g'], lp['nsa_k_norm_g'],
                                               bf16 if prompt else f32)
    kv4 = lambda v: v.reshape(bsz, -1, NSA_KV_HEADS, HEAD_DIM)
    if prompt:
        c_out = _nsa_attn_prompt(qn, qr, tail, kc, vc, ks, vs, kw, vw, lp['nsa_cmp_pe'], lp['nsa_cmp_w'], bsz, t)
        zpad = jnp.zeros((bsz, win_buf, NSA_KV_HEADS, HEAD_DIM), f32)
        kw_new = jnp.concatenate([zpad, kv4(kw)], axis=1)[:, t:]
        vw_new = jnp.concatenate([zpad, kv4(vw)], axis=1)[:, t:]
    else:
        flat3 = lambda v: v.reshape(-1, v.shape[-3], KV_DIM)
        n_phys = st['cmp_k'].shape[1]
        c_out, kw3, vw3 = _nsa_sample(qn, qr, tail, kc, vc, ks, vs, kw, vw, flat3(st['swa_k']), flat3(st['swa_v']),
                                      flat3(st['cmp_k']), flat3(st['cmp_v']), flat3(st['sel_k']), flat3(st['sel_v']),
                                      st['page_table'], lp['nsa_cmp_pe'], lp['nsa_cmp_w'], bsz, t, layer * n_phys)
        kw_new, vw_new = kv4(kw3), kv4(vw3)
    nsa_state = (kv4(kc), kv4(vc), kv4(ks), kv4(vs), kw_new, vw_new)

    mix = jnp.concatenate([a_out.reshape(m, -1).astype(bf16), b_out.reshape(m, -1).astype(bf16), c_out.astype(bf16)],
                          axis=-1)
    x2 = _matmul_w(mix, lp['w_out'], layer, residual=x2)
    h2 = _rmsnorm_bf16(x2, lp['ffn_norm_g'])
    gu = _matmul_w(h2, lp['w_up'], layer).reshape(bsz, t, 2 * D_FF)
    if prompt:
        act, new_f = _ffn_act(gu, buf_f, lp['ffn_conv_w'], lp['ffn_conv_b'], act_dtype, nseq=1, tc=FFN_TC)
    else:
        act, new_f = _ffn_act(gu, buf_f, lp['ffn_conv_w'], lp['ffn_conv_b'], act_dtype, nseq=bsz, tc=D_FF_HALF)
    act = act.reshape(m, D_FF).astype(bf16)
    x2 = _matmul_w(act, lp['w_down'], layer, residual=x2, tm=512, tk=D_FF_HALF, kblk=0)
    x2 = _matmul_w(act, lp['w_down'], layer, residual=x2, tm=512, tk=D_FF_HALF, kblk=1)
    h3 = _rmsnorm_bf16(x2, lp['ple_norm_g'])
    x2 = _ple(h3, lp['w_ple_gate'], p_emb.reshape(m, PLE_DIM), lp['w_ple_proj'], x2, layer)
    return x2.reshape(bsz, t, D_MODEL), nsa_state + (new_h, new_ssm_conv, new_a, new_f)


def _prep_weights(w_in):
    bf16 = jnp.bfloat16
    w_in_b = w_in[:, ORIG_Q:ORIG_GATES].astype(bf16)
    w_in_tail = jnp.concatenate([w_in[:, ORIG_DT:ORIG_Q], w_in[:, ORIG_GATES:],
                                 jnp.zeros((D_MODEL, U_TAIL - SSM_HEADS - 3 * NSA_HEADS), w_in.dtype)], axis=1).astype(bf16)
    return dict(w_in_b=w_in_b, w_in_tail=w_in_tail)


def kernel(x_prompt, x_sample, cache_cmp_k, cache_cmp_v, cache_sel_k, cache_sel_v, state_swa_k, state_swa_v,
           state_ssm, state_ssm_conv, state_conv_a, state_ffn_conv, page_table, p_prompt, p_sample,
           attn_norm_g, w_in, conv_a_w, conv_a_b, conv_a_ln_g, conv_a_ln_b, ssm_conv_w, ssm_conv_b,
           ssm_dt_bias, ssm_a_log, ssm_d, ssm_norm_g, nsa_q_norm_g, nsa_k_norm_g, nsa_cmp_pe, nsa_cmp_w,
           w_out, ffn_norm_g, w_up, ffn_conv_w, ffn_conv_b, w_down, ple_norm_g, w_ple_gate, w_ple_proj):
    past_len = page_table.shape[1] * PAGE_SIZE
    win_buf = state_swa_k.shape[2]
    pos_p = jnp.arange(x_prompt.shape[1], dtype=jnp.int32)
    pos_s = past_len + jnp.arange(x_sample.shape[1], dtype=jnp.int32)
    y_p, y_s = x_prompt, x_sample
    states_p, states_s = [], []
    for i in range(DEPTH):
        lp = {'attn_norm_g': attn_norm_g[i], 'conv_a_w': conv_a_w[i], 'conv_a_b': conv_a_b[i],
              'conv_a_ln_g': conv_a_ln_g[i], 'conv_a_ln_b': conv_a_ln_b[i], 'ssm_conv_w': ssm_conv_w[i],
              'ssm_conv_b': ssm_conv_b[i], 'ssm_dt_bias': ssm_dt_bias[i], 'ssm_a_log': ssm_a_log[i],
              'ssm_d': ssm_d[i], 'ssm_norm_g': ssm_norm_g[i], 'nsa_q_norm_g': nsa_q_norm_g[i],
              'nsa_k_norm_g': nsa_k_norm_g[i], 'nsa_cmp_pe': nsa_cmp_pe[i], 'nsa_cmp_w': nsa_cmp_w[i],
              'ffn_norm_g': ffn_norm_g[i], 'ple_norm_g': ple_norm_g[i],
              'w_in': w_in, 'w_out': w_out, 'w_up': w_up, 'w_down': w_down,
              'w_ple_gate': w_ple_gate, 'w_ple_proj': w_ple_proj,
              'ffn_conv_w': ffn_conv_w[i], 'ffn_conv_b': ffn_conv_b[i].reshape(1, D_FF)}
        lp.update(_prep_weights(w_in[i]))
        st = {'cmp_k': cache_cmp_k, 'cmp_v': cache_cmp_v, 'sel_k': cache_sel_k, 'sel_v': cache_sel_v,
              'swa_k': state_swa_k[i], 'swa_v': state_swa_v[i], 'ssm': state_ssm[i], 'ssm_conv': state_ssm_conv[i],
              'conv_a': state_conv_a[i], 'ffn_conv': state_ffn_conv[i], 'page_table': page_table}
        y_p, sp = _trunk_layer(y_p, p_prompt[i], pos_p, lp, None, win_buf, i)
        y_s, ss = _trunk_layer(y_s, p_sample[i], pos_s, lp, st, win_buf, i)
        states_p.append(sp)
        states_s.append(ss)
    (ck_p, cv_p, sk_p, sv_p, wk_p, wv_p, ssm_p, sc_p, ca_p, fc_p) = [jnp.stack(z) for z in zip(*states_p)]
    (ck_s, cv_s, sk_s, sv_s, wk_s, wv_s, ssm_s, sc_s, ca_s, fc_s) = [jnp.stack(z) for z in zip(*states_s)]
    return (y_p, y_s, ck_p, ck_s, cv_p, cv_s, sk_p, sk_s, sv_p, sv_s, wk_p, wk_s, wv_p, wv_s,
            ssm_p, ssm_s, sc_p, sc_s, ca_p, ca_s, fc_p, fc_s)
```

```python
import functools

import jax
import jax.numpy as jnp
from jax import lax
from jax.experimental import pallas as pl
from jax.experimental.pallas import tpu as pltpu

D_MODEL = 4096
DEPTH = 2
PAGE_SIZE = 128
PLE_DIM = 256
CONV_A_DIM = 1024
CONV_A_WIDTH = 31
SSM_DIM = 1024
SSM_HEADDIM = 64
SSM_HEADS = SSM_DIM // SSM_HEADDIM
SSM_GROUPS = 4
SSM_STATE = 128
SSM_CONV = 4
SSM_CHUNK = 128
SSM_CONV_DIM = SSM_DIM + 2 * SSM_GROUPS * SSM_STATE
NSA_HEADS = 16
NSA_KV_HEADS = 4
GQA = NSA_HEADS // NSA_KV_HEADS
HEAD_DIM = 128
NSA_DIM = NSA_HEADS * HEAD_DIM
KV_DIM = NSA_KV_HEADS * HEAD_DIM
CMP_BLOCK = 32
SEL_BLOCK = 64
CMP_PER_SEL = SEL_BLOCK // CMP_BLOCK
SEL_TOPK = 16
WINDOW = 512
SWA_QBLOCK = 128
ROPE_THETA = 10000.0
D_FF = 11008
FFN_CONV = 3
EPS = 1e-6
NEG = -1e30
FORCE = 1e9
ATTN_SCALE = HEAD_DIM ** -0.5

LANES = 128
SUBLANES = 8

OFF_CONV = 0
OFF_Z = OFF_CONV + 2 * CONV_A_DIM
OFF_XBC = OFF_Z + SSM_DIM
U_A = OFF_XBC + SSM_CONV_DIM
OFF_Q = 0
OFF_KV = OFF_Q + NSA_DIM
U_B = OFF_KV + 6 * KV_DIM
U_TAIL = LANES
TAIL_GATES = SSM_HEADS
ORIG_DT = U_A
ORIG_Q = ORIG_DT + SSM_HEADS
ORIG_GATES = ORIG_Q + U_B

D_FF_HALF = D_FF // 2
FFN_TC = 256
ROW_CHUNK = 256
CONV_A_HALO = 32
CONV_A_ROWS = 32
SSM_HALO = SUBLANES
NSA_TQ = 256
NSA_KC = 512
VMEM_LIMIT = 54 * 1024 * 1024

_NT = (((1,), (1,)), ((), ()))


def _cparams(sem):
    return pltpu.CompilerParams(dimension_semantics=sem, vmem_limit_bytes=VMEM_LIMIT)


def _dot(a, b):
    return jnp.dot(a, b, preferred_element_type=jnp.float32)


def _dot_nt(a, b):
    return lax.dot_general(a, b, _NT, preferred_element_type=jnp.float32)


def _dot_f32(a, b):
    return jnp.dot(a, b, preferred_element_type=jnp.float32, precision=lax.Precision.HIGHEST)


def _bf(x):
    return x.astype(jnp.bfloat16)


def _sigmoid(x):
    return jax.nn.sigmoid(x)


def _rmsnorm_kernel(x_ref, g_ref, o_ref):
    x = x_ref[...]
    ms = jnp.mean(x * x, axis=-1, keepdims=True)
    o_ref[...] = (x * lax.rsqrt(ms + EPS) * g_ref[...]).astype(o_ref.dtype)


def _rmsnorm_bf16(x, g):
    m, d = x.shape
    tm = min(m, 256)
    return pl.pallas_call(
        _rmsnorm_kernel,
        grid=(m // tm,),
        in_specs=[pl.BlockSpec((tm, d), lambda i: (i, 0)), pl.BlockSpec((1, d), lambda i: (0, 0))],
        out_specs=pl.BlockSpec((tm, d), lambda i: (i, 0)),
        out_shape=jax.ShapeDtypeStruct((m, d), jnp.bfloat16),
        compiler_params=_cparams(("parallel",)),
        name="rmsnorm",
    )(x, g.reshape(1, d))


def _mm_kernel(a_ref, b_ref, *rest, nk, has_res):
    if has_res:
        r_ref, o_ref, acc_ref = rest
    else:
        o_ref, acc_ref = rest
    k = pl.program_id(2)
    part = _dot(a_ref[...], b_ref[...])

    if nk == 1:
        o_ref[...] = part + r_ref[...] if has_res else part
        return

    @pl.when(k == 0)
    def _():
        acc_ref[...] = part

    @pl.when(k > 0)
    def _():
        acc_ref[...] += part

    @pl.when(k == nk - 1)
    def _():
        o_ref[...] = acc_ref[...] + r_ref[...] if has_res else acc_ref[...]


def _matmul(a, b, residual=None, *, tm=1024, tn=512, tk=None):
    m, kdim = a.shape
    _, n = b.shape
    tm = min(tm, m)
    tn = min(tn, n)
    tk = kdim if tk is None else tk
    nk = kdim // tk
    in_specs = [pl.BlockSpec((tm, tk), lambda i, j, k: (i, k)), pl.BlockSpec((tk, tn), lambda i, j, k: (k, j))]
    args = [a, b]
    if residual is not None:
        in_specs.append(pl.BlockSpec((tm, tn), lambda i, j, k: (i, j)))
        args.append(residual)
    return pl.pallas_call(
        functools.partial(_mm_kernel, nk=nk, has_res=residual is not None),
        grid=(m // tm, n // tn, nk),
        in_specs=in_specs,
        out_specs=pl.BlockSpec((tm, tn), lambda i, j, k: (i, j)),
        out_shape=jax.ShapeDtypeStruct((m, n), jnp.float32),
        scratch_shapes=[pltpu.VMEM((tm, tn) if nk > 1 else (SUBLANES, LANES), jnp.float32)],
        compiler_params=_cparams(("parallel", "parallel", "arbitrary")),
        name="matmul",
    )(*args)


def _mmw_kernel(a_ref, w_ref, *rest, has_res):
    if has_res:
        r_ref, o_ref, wb_ref = rest
    else:
        o_ref, wb_ref = rest

    @pl.when(pl.program_id(1) == 0)
    def _():
        wb_ref[...] = _bf(w_ref[...])

    part = _dot(a_ref[...], wb_ref[...])
    o_ref[...] = part + r_ref[...] if has_res else part


def _matmul_w(a, w, layer, residual=None, *, n_cols=None, tm=1024, tn=512, tk=None, kblk=0):
    m = a.shape[0]
    tk = w.shape[1] if tk is None else tk
    n = w.shape[2] if n_cols is None else n_cols
    tm = min(tm, m)
    in_specs = [pl.BlockSpec((tm, tk), lambda j, i: (i, kblk)),
                pl.BlockSpec((None, tk, tn), lambda j, i: (layer, kblk, j))]
    args = [a, w]
    if residual is not None:
        in_specs.append(pl.BlockSpec((tm, tn), lambda j, i: (i, j)))
        args.append(residual)
    return pl.pallas_call(
        functools.partial(_mmw_kernel, has_res=residual is not None),
        grid=(n // tn, m // tm),
        in_specs=in_specs,
        out_specs=pl.BlockSpec((tm, tn), lambda j, i: (i, j)),
        out_shape=jax.ShapeDtypeStruct((m, n), jnp.float32),
        scratch_shapes=[pltpu.VMEM((tk, tn), jnp.bfloat16)],
        compiler_params=_cparams(("parallel", "arbitrary")),
        name="matmul_w",
    )(*args)


def _ple_kernel(h_ref, wg_ref, p_ref, wp_ref, x_ref, o_ref, wgb_ref, wpb_ref):
    @pl.when(pl.program_id(1) == 0)
    def _():
        wgb_ref[...] = _bf(wg_ref[...])
        wpb_ref[...] = _bf(wp_ref[...])

    gate = _sigmoid(_dot(h_ref[...], wgb_ref[...]))
    proj = _dot(_bf(p_ref[...]), wpb_ref[...])
    o_ref[...] = x_ref[...] + proj * gate


def _ple(h, wg, p, wp, x, layer, *, tm=1024, tn=512):
    m, d = h.shape
    n = wg.shape[2]
    tm = min(tm, m)
    return pl.pallas_call(
        _ple_kernel,
        grid=(n // tn, m // tm),
        in_specs=[pl.BlockSpec((tm, d), lambda j, i: (i, 0)), pl.BlockSpec((None, d, tn), lambda j, i: (layer, 0, j)),
                  pl.BlockSpec((tm, PLE_DIM), lambda j, i: (i, 0)),
                  pl.BlockSpec((None, PLE_DIM, tn), lambda j, i: (layer, 0, j)),
                  pl.BlockSpec((tm, tn), lambda j, i: (i, j))],
        out_specs=pl.BlockSpec((tm, tn), lambda j, i: (i, j)),
        out_shape=jax.ShapeDtypeStruct((m, n), jnp.float32),
        scratch_shapes=[pltpu.VMEM((d, tn), jnp.bfloat16), pltpu.VMEM((PLE_DIM, tn), jnp.bfloat16)],
        compiler_params=_cparams(("parallel", "arbitrary")),
        name="ple",
    )(h, wg, p, wp, x)


def _ffn_act_kernel(g_ref, u_ref, buf_ref, w_ref, b_ref, act_ref, nb_ref, *, nseq, t, rc):
    w = w_ref[...]
    bias = b_ref[...]
    for s in range(nseq):
        buf = buf_ref[s]

        def body(ci, carry, s=s, buf=buf):
            r0 = pl.multiple_of(ci * rc, rc)
            cur = g_ref[s, pl.ds(r0, rc), :]
            p0 = pl.multiple_of(jnp.maximum(r0 - SUBLANES, 0), SUBLANES)
            prev = g_ref[s, pl.ds(p0, SUBLANES), :]
            first = ci == 0
            hm1 = jnp.where(first, buf[1:2], prev[SUBLANES - 1:SUBLANES])
            hm2 = jnp.where(first, buf[0:1], prev[SUBLANES - 2:SUBLANES - 1])
            row = lax.broadcasted_iota(jnp.int32, cur.shape, 0)
            g1 = jnp.where(row == 0, hm1, pltpu.roll(cur, 1, axis=0))
            g2 = jnp.where(row == 0, hm2, jnp.where(row == 1, hm1, pltpu.roll(cur, 2, axis=0)))
            c = w[0:1] * g2 + w[1:2] * g1 + w[2:3] * cur + bias
            act_ref[s, pl.ds(r0, rc), :] = (c * _sigmoid(c) * u_ref[s, pl.ds(r0, rc), :]).astype(act_ref.dtype)
            return carry

        lax.fori_loop(0, t // rc, body, 0)
        nb_ref[s] = g_ref[s, t - 2:t, :]


def _ffn_act(gu, buf, w, b, out_dtype, *, nseq, tc):
    bsz, t, _ = gu.shape
    nc = D_FF // tc
    rc = min(t, ROW_CHUNK)
    return pl.pallas_call(
        functools.partial(_ffn_act_kernel, nseq=nseq, t=t, rc=rc),
        grid=(bsz // nseq, nc),
        in_specs=[pl.BlockSpec((nseq, t, tc), lambda bi, j: (bi, 0, j)),
                  pl.BlockSpec((nseq, t, tc), lambda bi, j: (bi, 0, j + nc)),
                  pl.BlockSpec((nseq, FFN_CONV - 1, tc), lambda bi, j: (bi, 0, j)),
                  pl.BlockSpec((FFN_CONV, tc), lambda bi, j: (0, j)),
                  pl.BlockSpec((1, tc), lambda bi, j: (0, j))],
        out_specs=[pl.BlockSpec((nseq, t, tc), lambda bi, j: (bi, 0, j)),
                   pl.BlockSpec((nseq, FFN_CONV - 1, tc), lambda bi, j: (bi, 0, j))],
        out_shape=[jax.ShapeDtypeStruct((bsz, t, D_FF), out_dtype),
                   jax.ShapeDtypeStruct((bsz, FFN_CONV - 1, D_FF), jnp.float32)],
        compiler_params=_cparams(("parallel", "parallel")),
        name="ffn_act",
    )(gu, gu, buf, w, b)


def _ffn_up_kernel(a_ref, wg_ref, wu_ref, cw_ref, cb_ref, act_ref, nf_ref, wb_ref, gu_ref, *, tm, tn, rc, tiles_per_seq):
    i = pl.program_id(1)

    @pl.when(i == 0)
    def _():
        wb_ref[:, 0:tn] = _bf(wg_ref[...])
        wb_ref[:, tn:2 * tn] = _bf(wu_ref[...])

    gu_ref[SUBLANES:SUBLANES + tm, :] = _dot(a_ref[...], wb_ref[...])
    seq_start = (i % tiles_per_seq) == 0
    w = cw_ref[...]
    bias = cb_ref[...]

    def body(ci, carry):
        r0 = pl.multiple_of(ci * rc, rc)
        cur = gu_ref[pl.ds(SUBLANES + r0, rc), 0:tn]
        prev = gu_ref[pl.ds(r0, SUBLANES), 0:tn]
        fresh = seq_start & (ci == 0)
        hm1 = jnp.where(fresh, 0.0, prev[SUBLANES - 1:SUBLANES])
        hm2 = jnp.where(fresh, 0.0, prev[SUBLANES - 2:SUBLANES - 1])
        row = lax.broadcasted_iota(jnp.int32, cur.shape, 0)
        g1 = jnp.where(row == 0, hm1, pltpu.roll(cur, 1, axis=0))
        g2 = jnp.where(row == 0, hm2, jnp.where(row == 1, hm1, pltpu.roll(cur, 2, axis=0)))
        c = w[0:1] * g2 + w[1:2] * g1 + w[2:3] * cur + bias
        act_ref[pl.ds(r0, rc), :] = (c * _sigmoid(c) * gu_ref[pl.ds(SUBLANES + r0, rc), tn:2 * tn]).astype(act_ref.dtype)
        return carry

    lax.fori_loop(0, tm // rc, body, 0)
    last = gu_ref[tm:tm + SUBLANES, 0:tn]
    nf_ref[0] = last[SUBLANES - (FFN_CONV - 1):SUBLANES]
    gu_ref[0:SUBLANES, 0:tn] = last


def _ffn_up_act(h, w_up, layer, cw, cb, bsz, t, *, tm=1024, tn=FFN_TC):
    m, d = h.shape
    nc = D_FF // tn
    tiles_per_seq = t // tm
    return pl.pallas_call(
        functools.partial(_ffn_up_kernel, tm=tm, tn=tn, rc=ROW_CHUNK, tiles_per_seq=tiles_per_seq),
        grid=(nc, m // tm),
        in_specs=[pl.BlockSpec((tm, d), lambda j, i: (i, 0)),
                  pl.BlockSpec((None, d, tn), lambda j, i: (layer, 0, j)),
                  pl.BlockSpec((None, d, tn), lambda j, i: (layer, 0, j + nc)),
                  pl.BlockSpec((FFN_CONV, tn), lambda j, i: (0, j)),
                  pl.BlockSpec((1, tn), lambda j, i: (0, j))],
        out_specs=[pl.BlockSpec((tm, tn), lambda j, i: (i, j)),
                   pl.BlockSpec((1, FFN_CONV - 1, tn), lambda j, i: (i // tiles_per_seq, 0, j))],
        out_shape=[jax.ShapeDtypeStruct((m, D_FF), jnp.bfloat16),
                   jax.ShapeDtypeStruct((bsz, FFN_CONV - 1, D_FF), jnp.float32)],
        scratch_shapes=[pltpu.VMEM((d, 2 * tn), jnp.bfloat16), pltpu.VMEM((SUBLANES + tm, 2 * tn), jnp.float32)],
        compiler_params=_cparams(("parallel", "arbitrary")),
        name="ffn_up_act",
    )(h, w_up, w_up, cw, cb)


def _conformer_kernel(u_ref, buf_ref, w_ref, b_ref, lg_ref, lb_ref, o_ref, nb_ref, ext_ref, *, tt, rc):
    ti = pl.program_id(1)

    @pl.when(ti == 0)
    def _():
        ext_ref[0:CONV_A_HALO, :] = buf_ref[0]

    x = u_ref[0]
    ext_ref[CONV_A_HALO:CONV_A_HALO + tt, :] = x[:, :CONV_A_DIM] * _sigmoid(x[:, CONV_A_DIM:])
    bias = b_ref[...]
    lg = lg_ref[...]
    lb = lb_ref[...]
    first_tap = CONV_A_HALO - (CONV_A_WIDTH - 1)
    for r in range(tt // rc):
        r0 = r * rc
        acc = jnp.zeros((rc, CONV_A_DIM), jnp.float32) + bias
        for k in range(CONV_A_WIDTH):
            acc = acc + w_ref[k:k + 1, :] * ext_ref[r0 + first_tap + k:r0 + first_tap + k + rc, :]
        mu = jnp.mean(acc, axis=-1, keepdims=True)
        d = acc - mu
        var = jnp.mean(d * d, axis=-1, keepdims=True)
        y = d * lax.rsqrt(var + EPS) * lg + lb
        o_ref[0, r0:r0 + rc, :] = (y * _sigmoid(y)).astype(o_ref.dtype)
    carry = ext_ref[tt:tt + CONV_A_HALO, :]
    nb_ref[0] = carry
    ext_ref[0:CONV_A_HALO, :] = carry


def _conformer(u3, buf, w, b, lg, lb, out_dtype):
    bsz, t, _ = u3.shape
    tt = min(t, ROW_CHUNK)
    rc = min(tt, CONV_A_ROWS)
    vec = lambda: pl.BlockSpec((1, CONV_A_DIM), lambda bi, ti: (0, 0))
    return pl.pallas_call(
        functools.partial(_conformer_kernel, tt=tt, rc=rc),
        grid=(bsz, t // tt),
        in_specs=[pl.BlockSpec((1, tt, 2 * CONV_A_DIM), lambda bi, ti: (bi, ti, OFF_CONV // (2 * CONV_A_DIM))),
                  pl.BlockSpec((1, CONV_A_HALO, CONV_A_DIM), lambda bi, ti: (bi, 0, 0)),
                  pl.BlockSpec((CONV_A_WIDTH, CONV_A_DIM), lambda bi, ti: (0, 0)),
                  vec(), vec(), vec()],
        out_specs=[pl.BlockSpec((1, tt, CONV_A_DIM), lambda bi, ti: (bi, ti, 0)),
                   pl.BlockSpec((1, CONV_A_HALO, CONV_A_DIM), lambda bi, ti: (bi, 0, 0))],
        out_shape=[jax.ShapeDtypeStruct((bsz, t, CONV_A_DIM), out_dtype),
                   jax.ShapeDtypeStruct((bsz, CONV_A_HALO, CONV_A_DIM), jnp.float32)],
        scratch_shapes=[pltpu.VMEM((CONV_A_HALO + tt, CONV_A_DIM), jnp.float32)],
        compiler_params=_cparams(("parallel", "arbitrary")),
        name="conformer",
    )(u3, buf, w, b.reshape(1, -1), lg.reshape(1, -1), lb.reshape(1, -1))


def _ssd_kernel(z_ref, xbc_lo_ref, xbc_hi_ref, dt_ref, cbuf_ref, h0_ref, cw_ref, cb_ref, dtb_ref, alog_ref, dskip_ref,
                ng_ref, y_ref, hout_ref, cbout_ref, ext_ref, h_ref, ys_ref, *, valid):
    f32 = jnp.float32
    ln = SSM_CHUNK
    ci = pl.program_id(1)

    @pl.when(ci == 0)
    def _():
        ext_ref[0:SSM_HALO, :] = cbuf_ref[0]
        h_ref[...] = h0_ref[0]

    ext_ref[SSM_HALO:SSM_HALO + ln, 0:SSM_CONV_DIM // 2] = xbc_lo_ref[0]
    ext_ref[SSM_HALO:SSM_HALO + ln, SSM_CONV_DIM // 2:SSM_CONV_DIM] = xbc_hi_ref[0]
    conv = jnp.zeros((ln, SSM_CONV_DIM), f32) + cb_ref[...]
    for k in range(SSM_CONV):
        s0 = SSM_HALO - (SSM_CONV - 1) + k
        conv = conv + cw_ref[k:k + 1, :] * ext_ref[s0:s0 + ln, :]
    c = conv * _sigmoid(conv)
    gn = SSM_GROUPS * SSM_STATE

    row = lax.broadcasted_iota(jnp.int32, (ln, LANES), 0)
    lane = lax.broadcasted_iota(jnp.int32, (ln, LANES), 1)
    xr = dt_ref[0] + dtb_ref[...]
    dt = jnp.maximum(xr, 0.0) + jnp.log1p(jnp.exp(-jnp.abs(xr)))
    if valid < ln:
        dt = jnp.where(row < valid, dt, 0.0)
    la = dt * (-jnp.exp(alog_ref[...]))
    tril = jnp.where(row >= lane, 1.0, 0.0).astype(f32)
    acum = _dot_f32(tril, la)
    acum_t = acum.T
    alast = acum[ln - 1:ln, :]
    causal = row >= lane
    lane_lo = lane < SSM_HEADDIM
    pair_w = 2 * SSM_HEADDIM

    cb_cache = {}
    for j in range(SSM_HEADS // 2):
        h0i, h1i = 2 * j, 2 * j + 1
        g = h0i // (SSM_HEADS // SSM_GROUPS)
        bm = _bf(c[:, SSM_DIM + g * SSM_STATE:SSM_DIM + (g + 1) * SSM_STATE])
        cm = _bf(c[:, SSM_DIM + gn + g * SSM_STATE:SSM_DIM + gn + (g + 1) * SSM_STATE])
        if g not in cb_cache:
            cb_cache[g] = _dot_nt(cm, bm)
        cbg = cb_cache[g]
        a0, a1 = acum[:, h0i:h0i + 1], acum[:, h1i:h1i + 1]
        dec0 = jnp.where(causal, jnp.exp(a0 - acum_t[h0i:h0i + 1, :]), 0.0)
        dec1 = jnp.where(causal, jnp.exp(a1 - acum_t[h1i:h1i + 1, :]), 0.0)
        sc = jnp.concatenate([_bf(cbg * dec0), _bf(cbg * dec1)], axis=1)
        xs = c[:, j * pair_w:(j + 1) * pair_w]
        xdt = xs * jnp.where(lane_lo, dt[:, h0i:h0i + 1], dt[:, h1i:h1i + 1])
        xblk = jnp.concatenate([_bf(jnp.where(lane_lo, xdt, 0.0)), _bf(jnp.where(lane_lo, 0.0, xdt))], axis=0)
        y_diag = _dot(sc, xblk)
        hp = h_ref[j * pair_w:(j + 1) * pair_w, :]
        y_off = _dot_nt(cm, _bf(hp)) * jnp.where(lane_lo, jnp.exp(a0), jnp.exp(a1))
        ys_ref[:, j * pair_w:(j + 1) * pair_w] = y_diag + y_off + xs * dskip_ref[:, j * pair_w:(j + 1) * pair_w]
        al0, al1 = alast[:, h0i:h0i + 1], alast[:, h1i:h1i + 1]
        dend = jnp.where(lane_lo, jnp.exp(al0 - a0), jnp.exp(al1 - a1))
        s_new = _dot(_bf((xdt * dend).T), bm)
        h_ref[j * pair_w:(j + 1) * pair_w, :] = jnp.where(row < SSM_HEADDIM, jnp.exp(al0), jnp.exp(al1)) * hp + s_new

    z = z_ref[0]
    y = ys_ref[...] * (z * _sigmoid(z))
    gw = SSM_DIM // SSM_GROUPS
    for g in range(SSM_GROUPS):
        yg = y[:, g * gw:(g + 1) * gw]
        ms = jnp.mean(yg * yg, axis=-1, keepdims=True)
        y_ref[0, :, g * gw:(g + 1) * gw] = (yg * lax.rsqrt(ms + EPS) * ng_ref[:, g * gw:(g + 1) * gw]).astype(y_ref.dtype)

    tail = ext_ref[valid:valid + SSM_HALO, :]
    cbout_ref[0] = tail
    ext_ref[0:SSM_HALO, :] = tail
    hout_ref[0] = h_ref[...]


def _ssd(u3, tail3, cbuf, h0, cw, cb, dtb, alog, dskip, ng, valid, out_dtype):
    bsz, t, _ = u3.shape
    ln = SSM_CHUNK
    half = SSM_CONV_DIM // 2
    full = lambda shape: pl.BlockSpec(shape, lambda bi, ci: (0,) * len(shape))
    return pl.pallas_call(
        functools.partial(_ssd_kernel, valid=valid),
        grid=(bsz, t // ln),
        in_specs=[pl.BlockSpec((1, ln, SSM_DIM), lambda bi, ci: (bi, ci, OFF_Z // SSM_DIM)),
                  pl.BlockSpec((1, ln, half), lambda bi, ci: (bi, ci, OFF_XBC // half)),
                  pl.BlockSpec((1, ln, half), lambda bi, ci: (bi, ci, OFF_XBC // half + 1)),
                  pl.BlockSpec((1, ln, U_TAIL), lambda bi, ci: (bi, ci, 0)),
                  pl.BlockSpec((1, SSM_HALO, SSM_CONV_DIM), lambda bi, ci: (bi, 0, 0)),
                  pl.BlockSpec((1, SSM_DIM, SSM_STATE), lambda bi, ci: (bi, 0, 0)),
                  full((SSM_CONV, SSM_CONV_DIM)), full((1, SSM_CONV_DIM)), full((1, LANES)), full((1, LANES)),
                  full((1, SSM_DIM)), full((1, SSM_DIM))],
        out_specs=[pl.BlockSpec((1, ln, SSM_DIM), lambda bi, ci: (bi, ci, 0)),
                   pl.BlockSpec((1, SSM_DIM, SSM_STATE), lambda bi, ci: (bi, 0, 0)),
                   pl.BlockSpec((1, SSM_HALO, SSM_CONV_DIM), lambda bi, ci: (bi, 0, 0))],
        out_shape=[jax.ShapeDtypeStruct((bsz, t, SSM_DIM), out_dtype),
                   jax.ShapeDtypeStruct((bsz, SSM_DIM, SSM_STATE), jnp.float32),
                   jax.ShapeDtypeStruct((bsz, SSM_HALO, SSM_CONV_DIM), jnp.float32)],
        scratch_shapes=[pltpu.VMEM((SSM_HALO + ln, SSM_CONV_DIM), jnp.float32),
                        pltpu.VMEM((SSM_DIM, SSM_STATE), jnp.float32),
                        pltpu.VMEM((ln, SSM_DIM), jnp.float32)],
        compiler_params=_cparams(("parallel", "arbitrary")),
        name="ssd",
    )(u3, u3, u3, tail3, cbuf, h0, cw, cb, dtb, alog, dskip, ng)


def _nsa_prep_kernel(q_ref, kvc_ref, kvs_ref, kvw_ref, cos_ref, sin_ref, qg_ref, kg_ref,
                     qn_ref, qr_ref, kc_ref, vc_ref, ks_ref, vs_ref, kw_ref, vw_ref):
    cos = cos_ref[...]
    sin = sin_ref[...]

    def norm(x, g):
        return x * lax.rsqrt(jnp.mean(x * x, axis=-1, keepdims=True) + EPS) * g

    def rope(x):
        return x * cos + pltpu.roll(x, HEAD_DIM // 2, axis=1) * sin

    qg = qg_ref[...]
    for h in range(NSA_HEADS):
        sl = slice(h * HEAD_DIM, (h + 1) * HEAD_DIM)
        x = norm(q_ref[:, sl], qg)
        qn_ref[:, sl] = x.astype(qn_ref.dtype)
        qr_ref[:, sl] = rope(x).astype(qr_ref.dtype)
    for h in range(NSA_KV_HEADS):
        sl = slice(h * HEAD_DIM, (h + 1) * HEAD_DIM)
        sv = slice(KV_DIM + h * HEAD_DIM, KV_DIM + (h + 1) * HEAD_DIM)
        kc_ref[:, sl] = norm(kvc_ref[:, sl], kg_ref[0:1, :])
        vc_ref[:, sl] = kvc_ref[:, sv]
        ks_ref[:, sl] = rope(norm(kvs_ref[:, sl], kg_ref[1:2, :]))
        vs_ref[:, sl] = kvs_ref[:, sv]
        kw_ref[:, sl] = rope(norm(kvw_ref[:, sl], kg_ref[2:3, :]))
        vw_ref[:, sl] = kvw_ref[:, sv]


def _nsa_prep(u, cos, sin, qg, kg, q_dtype):
    m = u.shape[0]
    tt = min(m, ROW_CHUNK)
    row = lambda w, blk=0: pl.BlockSpec((tt, w), lambda i: (i, blk))
    kv = jax.ShapeDtypeStruct((m, KV_DIM), jnp.float32)
    qo = jax.ShapeDtypeStruct((m, NSA_DIM), q_dtype)
    kv0 = OFF_KV // (2 * KV_DIM)
    return pl.pallas_call(
        _nsa_prep_kernel,
        grid=(m // tt,),
        in_specs=[row(NSA_DIM, OFF_Q // NSA_DIM), row(2 * KV_DIM, kv0), row(2 * KV_DIM, kv0 + 1), row(2 * KV_DIM, kv0 + 2),
                  row(HEAD_DIM), row(HEAD_DIM),
                  pl.BlockSpec((1, HEAD_DIM), lambda i: (0, 0)), pl.BlockSpec((3, HEAD_DIM), lambda i: (0, 0))],
        out_specs=[row(NSA_DIM), row(NSA_DIM)] + [row(KV_DIM)] * 6,
        out_shape=[qo, qo] + [kv] * 6,
        compiler_params=_cparams(("parallel",)),
        name="nsa_prep",
    )(u, u, u, u, cos, sin, qg.reshape(1, HEAD_DIM), kg)


def _softmax_rows(s):
    m = jnp.max(s, axis=-1, keepdims=True)
    e = jnp.exp(s - m)
    return e / jnp.sum(e, axis=-1, keepdims=True)


def _nsa_attn_kernel(qn_ref, qr_ref, tail_ref, kc_ref, vc_ref, ks_ref, vs_ref, kw_ref, vw_ref, pe_ref, cw_ref,
                     o_ref, kcmp_ref, kcmpp_ref, vcmp_ref, ksb_ref, vsb_ref, kwb_ref, vwb_ref,
                     m_ref, l_ref, acc_ref, *, t, tq):
    f32 = jnp.float32
    kvh = pl.program_id(1)
    qi = pl.program_id(2)
    ncb = t // CMP_BLOCK
    nsb = t // SEL_BLOCK
    rows = GQA * tq
    span = min(WINDOW + tq, t)
    cmp_shift = CMP_BLOCK.bit_length() - 1
    sel_shift = SEL_BLOCK.bit_length() - 1

    @pl.when(qi == 0)
    def _():
        r = lax.broadcasted_iota(jnp.int32, (ncb, t), 0)
        cblk = lax.broadcasted_iota(jnp.int32, (ncb, t), 1) >> cmp_shift
        avg_nat = jnp.where(cblk == r, 1.0 / CMP_BLOCK, 0.0).astype(f32)
        perm = jnp.where(r < nsb, 2 * r, 2 * (r - nsb) + 1)
        avg_perm = jnp.where(cblk == perm, 1.0 / CMP_BLOCK, 0.0).astype(f32)
        kc = kc_ref[...]
        pe_k = jnp.mean(pe_ref[0], axis=0, keepdims=True)
        pe_v = jnp.mean(pe_ref[1], axis=0, keepdims=True)
        wk = _bf(cw_ref[0])
        wv = _bf(cw_ref[1])
        kcmp_ref[...] = _bf(_dot(_bf(_dot_f32(avg_nat, kc) + pe_k), wk))
        kcmpp_ref[...] = _bf(_dot(_bf(_dot_f32(avg_perm, kc) + pe_k), wk))
        vcmp_ref[...] = _bf(_dot(_bf(_dot_f32(avg_nat, vc_ref[...]) + pe_v), wv))
        ksb_ref[...] = _bf(ks_ref[...])
        vsb_ref[...] = _bf(vs_ref[...])
        kwb_ref[...] = _bf(kw_ref[...])
        vwb_ref[...] = _bf(vw_ref[...])

    t0 = qi * tq

    def stack(ref):
        return jnp.concatenate([ref[:, g * HEAD_DIM:(g + 1) * HEAD_DIM] for g in range(GQA)], axis=0)

    qn = stack(qn_ref)

    tpos_c = t0 + (lax.broadcasted_iota(jnp.int32, (rows, ncb), 0) & (tq - 1))
    blk_end = (lax.broadcasted_iota(jnp.int32, (rows, ncb), 1) + 1) * CMP_BLOCK - 1
    s = jnp.where(blk_end <= tpos_c, _dot_nt(qn, kcmp_ref[...]) * ATTN_SCALE, NEG)
    anyvis = jnp.where(tpos_c[:, 0:1] >= CMP_BLOCK - 1, 1.0, 0.0).astype(f32)
    p = _softmax_rows(s) * anyvis
    o_cmp = _dot(_bf(p), vcmp_ref[...])

    rperm = lax.broadcasted_iota(jnp.int32, (ncb, rows), 0)
    blk_t = jnp.where(rperm < nsb, 2 * rperm, 2 * (rperm - nsb) + 1)
    tpos_t = t0 + (lax.broadcasted_iota(jnp.int32, (ncb, rows), 1) & (tq - 1))
    st = jnp.where((blk_t + 1) * CMP_BLOCK - 1 <= tpos_t, _dot_nt(kcmpp_ref[...], qn) * ATTN_SCALE, NEG)
    mt = jnp.max(st, axis=0, keepdims=True)
    et = jnp.exp(st - mt)
    pt = et / jnp.sum(et, axis=0, keepdims=True) * jnp.where(tpos_t[0:1, :] >= CMP_BLOCK - 1, 1.0, 0.0).astype(f32)
    psum = pt[:, 0:tq]
    for g in range(1, GQA):
        psum = psum + pt[:, g * tq:(g + 1) * tq]
    imp = psum[0:nsb, :] + psum[nsb:2 * nsb, :]
    jrow = lax.broadcasted_iota(jnp.int32, (nsb, tq), 0)
    qp = t0 + lax.broadcasted_iota(jnp.int32, (nsb, tq), 1)
    forced = (jrow == (qp >> sel_shift)) | (jrow == 0)
    imp = jnp.where(forced, FORCE, jnp.where(jrow * SEL_BLOCK > qp, NEG, imp))
    cnt = jnp.zeros((nsb, tq), f32)
    for i in range(nsb):
        ri = imp[i:i + 1, :]
        cnt = cnt + jnp.where((ri > imp) | ((ri == imp) & (jrow > i)), 1.0, 0.0)
    sel_t = jnp.where(cnt < SEL_TOPK, 1.0, 0.0).astype(f32)
    sel_pad = jnp.concatenate([sel_t, jnp.zeros((LANES - nsb, tq), f32)], axis=0) if nsb < LANES else sel_t
    sel_q = _bf(sel_pad.T)

    kc_sz = min(NSA_KC, t)
    m_ref[...] = jnp.full((rows, 1), NEG, f32)
    l_ref[...] = jnp.zeros((rows, 1), f32)
    acc_ref[...] = jnp.zeros((rows, HEAD_DIM), f32)
    qpos_s = t0 + lax.broadcasted_iota(jnp.int32, (tq, kc_sz), 0)

    def sel_body(ck, carry):
        k0 = pl.multiple_of(ck * kc_sz, kc_sz)
        kb = ksb_ref[pl.ds(k0, kc_sz), :]
        vb = vsb_ref[pl.ds(k0, kc_sz), :]
        kpos = k0 + lax.broadcasted_iota(jnp.int32, (tq, kc_sz), 1)
        eblk = (k0 + lax.broadcasted_iota(jnp.int32, (LANES, kc_sz), 1)) >> sel_shift
        expand = jnp.where(eblk == lax.broadcasted_iota(jnp.int32, (LANES, kc_sz), 0), 1.0, 0.0).astype(jnp.bfloat16)
        chosen = _dot(sel_q, expand)
        visible = (chosen > 0.5) & (kpos <= qpos_s)
        for g in range(GQA):
            rs = slice(g * tq, (g + 1) * tq)
            sc = jnp.where(visible, _dot_nt(qr_ref[:, g * HEAD_DIM:(g + 1) * HEAD_DIM], kb) * ATTN_SCALE, NEG)
            m_old = m_ref[rs, :]
            m_new = jnp.maximum(m_old, jnp.max(sc, axis=-1, keepdims=True))
            alpha = jnp.exp(m_old - m_new)
            pc = jnp.exp(sc - m_new)
            l_ref[rs, :] = alpha * l_ref[rs, :] + jnp.sum(pc, axis=-1, keepdims=True)
            acc_ref[rs, :] = alpha * acc_ref[rs, :] + _dot(_bf(pc), vb)
            m_ref[rs, :] = m_new
        return carry

    lax.fori_loop(0, (t0 + tq + kc_sz - 1) // kc_sz, sel_body, 0)

    ws = pl.multiple_of(jnp.maximum(t0 + tq - span, 0), SWA_QBLOCK)
    kb = kwb_ref[pl.ds(ws, span), :]
    vb = vwb_ref[pl.ds(ws, span), :]
    diff = (t0 + lax.broadcasted_iota(jnp.int32, (tq, span), 0)) - (ws + lax.broadcasted_iota(jnp.int32, (tq, span), 1))
    win_vis = (diff >= 0) & (diff <= WINDOW)
    sg = _sigmoid(tail_ref[...])
    glane = lax.broadcasted_iota(jnp.int32, (tq, U_TAIL), 1)
    for g in range(GQA):
        rs = slice(g * tq, (g + 1) * tq)
        hs = slice(g * HEAD_DIM, (g + 1) * HEAD_DIM)
        sw = jnp.where(win_vis, _dot_nt(qr_ref[:, hs], kb) * ATTN_SCALE, NEG)
        o_swa = _dot(_bf(_softmax_rows(sw)), vb)
        o_sel = acc_ref[rs, :] / l_ref[rs, :]
        lane0 = TAIL_GATES + (kvh * GQA + g) * 3
        gate = lambda br: jnp.sum(jnp.where(glane == lane0 + br, sg, 0.0), axis=-1, keepdims=True)
        o_ref[:, hs] = (gate(0) * o_cmp[rs] + gate(1) * o_sel + gate(2) * o_swa).astype(o_ref.dtype)


def _nsa_attn_prompt(qn, qr, tail, kc, vc, ks, vs, kw, vw, pe, cw, bsz, t):
    tq = NSA_TQ
    nq = t // tq
    ncb = t // CMP_BLOCK
    rows = GQA * tq
    qspec = pl.BlockSpec((tq, GQA * HEAD_DIM), lambda b, k, q: (b * nq + q, k))
    kvspec = pl.BlockSpec((t, HEAD_DIM), lambda b, k, q: (b, k))
    bf16 = jnp.bfloat16
    return pl.pallas_call(
        functools.partial(_nsa_attn_kernel, t=t, tq=tq),
        grid=(bsz, NSA_KV_HEADS, nq),
        in_specs=[qspec, qspec, pl.BlockSpec((tq, U_TAIL), lambda b, k, q: (b * nq + q, 0))] + [kvspec] * 6 + [
            pl.BlockSpec((2, CMP_BLOCK, HEAD_DIM), lambda b, k, q: (0, 0, 0)),
            pl.BlockSpec((2, HEAD_DIM, HEAD_DIM), lambda b, k, q: (0, 0, 0))],
        out_specs=qspec,
        out_shape=jax.ShapeDtypeStruct((bsz * t, NSA_DIM), bf16),
        scratch_shapes=[pltpu.VMEM((ncb, HEAD_DIM), bf16), pltpu.VMEM((ncb, HEAD_DIM), bf16),
                        pltpu.VMEM((ncb, HEAD_DIM), bf16)] + [pltpu.VMEM((t, HEAD_DIM), bf16)] * 4 + [
                        pltpu.VMEM((rows, 1), jnp.float32), pltpu.VMEM((rows, 1), jnp.float32),
                        pltpu.VMEM((rows, HEAD_DIM), jnp.float32)],
        compiler_params=_cparams(("parallel", "parallel", "arbitrary")),
        name="nsa_attn",
    )(qn, qr, tail, kc, vc, ks, vs, kw, vw, pe, cw)


NSA_PP = 8


def _nsa_sample_kernel(pt_ref, qn_ref, qr_ref, tail_ref, kcn_ref, vcn_ref, ksn_ref, vsn_ref, kwn_ref, vwn_ref,
                       swk_ref, swv_ref, pe_ref, cw_ref, *rest, tq, past_len, ns):
    pp = NSA_PP
    ck_refs, cv_refs = rest[0:pp], rest[pp:2 * pp]
    sk_refs, sv_refs = rest[2 * pp:3 * pp], rest[3 * pp:4 * pp]
    (o_ref, swk_out_ref, swv_out_ref,
     kcmp_ref, vcmp_ref, sel_ref, ocmp_ref, m_ref, l_ref, acc_ref) = rest[4 * pp:]
    f32 = jnp.float32
    kvh_n = NSA_KV_HEADS
    s = pl.program_id(1)
    rows = GQA * tq
    blk_per_page = PAGE_SIZE // CMP_BLOCK
    blk_rows = CMP_BLOCK * kvh_n
    step_rows = pp * blk_per_page * kvh_n
    ncb_past = past_len // CMP_BLOCK
    nsb = past_len // SEL_BLOCK + 1
    sel_shift = SEL_BLOCK.bit_length() - 1
    hd = lambda k: slice(k * HEAD_DIM, (k + 1) * HEAD_DIM)
    row8 = lax.broadcasted_iota(jnp.int32, (SUBLANES, HEAD_DIM), 0)

    def pe_mean(i):
        return jnp.mean(pe_ref[i], axis=0, keepdims=True)

    def stack_q(ref, k):
        return _bf(jnp.concatenate([ref[:, hd(k * GQA + g)] for g in range(GQA)], axis=0))

    def tile_rows(x):
        return jnp.concatenate([x] * GQA, axis=0)

    def head_rows(ref, k, n):
        return ref[pl.ds(k, n, stride=kvh_n), :]

    @pl.when(s == 0)
    def _():
        kcmp_ref[ncb_past * kvh_n:, :] = jnp.zeros((kcmp_ref.shape[0] - ncb_past * kvh_n, HEAD_DIM), f32)
        vcmp_ref[ncb_past * kvh_n:, :] = jnp.zeros((vcmp_ref.shape[0] - ncb_past * kvh_n, HEAD_DIM), f32)

    @pl.when(s < ns)
    def _():
        def block_sums(ref):
            sums = []
            for c in range(blk_per_page):
                x = ref[c * blk_rows:(c + 1) * blk_rows, :]
                s8 = jnp.sum(x.reshape(blk_rows // SUBLANES, SUBLANES, HEAD_DIM), axis=0)
                sums.append(s8 + pltpu.roll(s8, kvh_n, axis=0))
            return [jnp.where(row8 < kvh_n, sums[c], sums[c + 1]) for c in range(0, blk_per_page, 2)]

        ktiles, vtiles = [], []
        for r in range(pp):
            ktiles += block_sums(ck_refs[r])
            vtiles += block_sums(cv_refs[r])
        mk = jnp.concatenate(ktiles, axis=0) * (1.0 / CMP_BLOCK) + pe_mean(0)
        mv = jnp.concatenate(vtiles, axis=0) * (1.0 / CMP_BLOCK) + pe_mean(1)
        r0 = pl.multiple_of(s * step_rows, step_rows)
        kcmp_ref[pl.ds(r0, step_rows), :] = _dot(_bf(mk), _bf(cw_ref[0]))
        vcmp_ref[pl.ds(r0, step_rows), :] = _dot(_bf(mv), _bf(cw_ref[1]))

    @pl.when(s == ns - 1)
    def _():
        ncp = sel_ref.shape[1]

        def new_tile(new_ref, pe_i):
            pm = pe_mean(pe_i)
            m_new = jnp.sum(new_ref[...], axis=0, keepdims=True) * (1.0 / CMP_BLOCK)
            tile = jnp.where(row8 >= kvh_n, pm, 0.0)
            for k in range(kvh_n):
                tile = jnp.where(row8 == k, m_new[:, hd(k)] + pm, tile)
            return tile

        kcmp_ref[ncb_past * kvh_n:ncb_past * kvh_n + SUBLANES, :] = _dot(_bf(new_tile(kcn_ref, 0)), _bf(cw_ref[0]))
        vcmp_ref[ncb_past * kvh_n:ncb_past * kvh_n + SUBLANES, :] = _dot(_bf(new_tile(vcn_ref, 1)), _bf(cw_ref[1]))

        lane = lax.broadcasted_iota(jnp.int32, (rows, ncp), 1)
        qpos = past_len + (lax.broadcasted_iota(jnp.int32, (rows, ncp), 0) & (tq - 1))
        vis = (lane < ncb_past + 2) & ((lane + 1) * CMP_BLOCK - 1 <= qpos)
        anyvis = jnp.where(qpos[:, 0:1] >= CMP_BLOCK - 1, 1.0, 0.0).astype(f32)
        lane_t = lax.broadcasted_iota(jnp.int32, (tq, ncp), 1)
        imps = []
        for k in range(kvh_n):
            kk = _bf(head_rows(kcmp_ref, k, ncp))
            vv = _bf(head_rows(vcmp_ref, k, ncp))
            p = _softmax_rows(jnp.where(vis, _dot_nt(stack_q(qn_ref, k), kk) * ATTN_SCALE, NEG)) * anyvis
            ocmp_ref[k * rows:(k + 1) * rows, :] = _dot(_bf(p), vv)
            psum = p[0:tq]
            for g in range(1, GQA):
                psum = psum + p[g * tq:(g + 1) * tq]
            imps.append(psum + jnp.where((lane_t & 1) == 0, pltpu.roll(psum, ncp - 1, axis=1), pltpu.roll(psum, 1, axis=1)))
        imp = jnp.concatenate(imps, axis=0)
        nrow = kvh_n * tq
        j2 = lax.broadcasted_iota(jnp.int32, (nrow, ncp), 1)
        j = j2 >> 1
        real = ((j2 & 1) == 0) & (j < nsb)
        qp = past_len + (lax.broadcasted_iota(jnp.int32, (nrow, ncp), 0) & (tq - 1))
        forced = (j == (qp >> sel_shift)) | (j == 0)
        imp = jnp.where(forced, FORCE, jnp.where(j * SEL_BLOCK > qp, NEG, imp))
        imp = jnp.where(real, imp, 2.0 * NEG)
        cnt = jnp.zeros((nrow, ncp), f32)
        for i in range(nsb):
            ci = imp[:, 2 * i:2 * i + 1]
            cnt = cnt + jnp.where((ci > imp) | ((ci == imp) & (j2 > 2 * i)), 1.0, 0.0)
        sel_ref[...] = jnp.where((cnt < min(SEL_TOPK, nsb)) & real, 1.0, 0.0).astype(f32)
        m_ref[...] = jnp.full(m_ref.shape, NEG, f32)
        l_ref[...] = jnp.zeros(l_ref.shape, f32)
        acc_ref[...] = jnp.zeros(acc_ref.shape, f32)

    def online_update(k, sc, vb):
        rs = slice(k * rows, (k + 1) * rows)
        m_old = m_ref[rs, :]
        m_new = jnp.maximum(m_old, jnp.max(sc, axis=-1, keepdims=True))
        alpha = jnp.exp(m_old - m_new)
        pc = jnp.exp(sc - m_new)
        l_ref[rs, :] = alpha * l_ref[rs, :] + jnp.sum(pc, axis=-1, keepdims=True)
        acc_ref[rs, :] = alpha * acc_ref[rs, :] + _dot(_bf(pc), vb)
        m_ref[rs, :] = m_new

    @pl.when(s >= ns)
    def _():
        s2 = s - ns
        nkeys = pp * PAGE_SIZE
        ncp = sel_ref.shape[1]
        jrow = lax.broadcasted_iota(jnp.int32, (ncp, nkeys), 0)
        kblk = s2 * (nkeys // SEL_BLOCK) + (lax.broadcasted_iota(jnp.int32, (ncp, nkeys), 1) >> sel_shift)
        expand = jnp.where(jrow == 2 * kblk, 1.0, 0.0).astype(jnp.bfloat16)
        chosen = _dot(_bf(sel_ref[...]), expand)
        kpos = s2 * nkeys + lax.broadcasted_iota(jnp.int32, (tq, nkeys), 1)
        qpos = past_len + lax.broadcasted_iota(jnp.int32, (tq, nkeys), 0)
        for k in range(kvh_n):
            kcat = jnp.concatenate([head_rows(sk_refs[r], k, PAGE_SIZE) for r in range(pp)], axis=0)
            vcat = jnp.concatenate([head_rows(sv_refs[r], k, PAGE_SIZE) for r in range(pp)], axis=0)
            hidden = tile_rows(jnp.where((chosen[k * tq:(k + 1) * tq] > 0.5) & (kpos <= qpos), 0.0, 1.0))
            sc = jnp.where(hidden > 0.5, NEG, _dot_nt(stack_q(qr_ref, k), _bf(kcat)) * ATTN_SCALE)
            online_update(k, sc, _bf(vcat))

    @pl.when(s == 2 * ns - 1)
    def _():
        npad = 2 * SUBLANES
        zpad = jnp.zeros((npad - tq, KV_DIM), f32)
        pad_new = lambda ref: jnp.concatenate([ref[...], zpad], axis=0)
        ksn, vsn, kwn, vwn = pad_new(ksn_ref), pad_new(vsn_ref), pad_new(kwn_ref), pad_new(vwn_ref)
        win = swk_ref.shape[1]
        trow = lax.broadcasted_iota(jnp.int32, (rows, npad), 0) & (tq - 1)
        ncol = lax.broadcasted_iota(jnp.int32, (rows, npad), 1)
        new_vis = (ncol <= trow) & (ncol < tq)
        wdiff = (past_len + (lax.broadcasted_iota(jnp.int32, (rows, win), 0) & (tq - 1))) - (
            past_len - win + lax.broadcasted_iota(jnp.int32, (rows, win), 1))
        win_vis = (wdiff >= 0) & (wdiff <= WINDOW)
        sg = _sigmoid(tail_ref[...])
        for k in range(NSA_KV_HEADS):
            qr = stack_q(qr_ref, k)
            picked = tile_rows(sel_ref[k * tq:(k + 1) * tq, 2 * (nsb - 1):2 * (nsb - 1) + 1]) > 0.5
            sc = jnp.where(picked & new_vis, _dot_nt(qr, _bf(ksn[:, hd(k)])) * ATTN_SCALE, NEG)
            online_update(k, sc, _bf(vsn[:, hd(k)]))
            rs = slice(k * rows, (k + 1) * rows)
            o_sel = acc_ref[rs, :] / l_ref[rs, :]
            s1 = jnp.where(win_vis, _dot_nt(qr, _bf(swk_ref[0, :, hd(k)])) * ATTN_SCALE, NEG)
            s2n = jnp.where(new_vis, _dot_nt(qr, _bf(kwn[:, hd(k)])) * ATTN_SCALE, NEG)
            mx = jnp.maximum(jnp.max(s1, axis=-1, keepdims=True), jnp.max(s2n, axis=-1, keepdims=True))
            e1 = jnp.exp(s1 - mx)
            e2 = jnp.exp(s2n - mx)
            den = jnp.sum(e1, axis=-1, keepdims=True) + jnp.sum(e2, axis=-1, keepdims=True)
            o_swa = _dot(_bf(e1 / den), _bf(swv_ref[0, :, hd(k)])) + _dot(_bf(e2 / den), _bf(vwn[:, hd(k)]))
            o_cmp = ocmp_ref[rs, :]
            for g in range(GQA):
                c0 = TAIL_GATES + (k * GQA + g) * 3
                gs = slice(g * tq, (g + 1) * tq)
                o = (sg[:, c0:c0 + 1] * o_cmp[gs] + sg[:, c0 + 1:c0 + 2] * o_sel[gs] + sg[:, c0 + 2:c0 + 3] * o_swa[gs])
                o_ref[:, hd(k * GQA + g)] = o
        swk_out_ref[0, 0:win - tq, :] = swk_ref[0, tq:win, :]
        swk_out_ref[0, win - tq:win, :] = kwn_ref[...]
        swv_out_ref[0, 0:win - tq, :] = swv_ref[0, tq:win, :]
        swv_out_ref[0, win - tq:win, :] = vwn_ref[...]


def _nsa_sample(qn, qr, tail, kc, vc, ks, vs, kw, vw, swa_k, swa_v, cache_ck, cache_cv, cache_sk, cache_sv,
                page_table, pe, cw, bsz, tq, page0):
    n_pages = page_table.shape[1]
    past_len = n_pages * PAGE_SIZE
    pp = NSA_PP
    ns = n_pages // pp
    ncp = -(-(past_len // CMP_BLOCK + CMP_PER_SEL) // LANES) * LANES
    win = swa_k.shape[1]
    rows = GQA * tq
    page_rows = PAGE_SIZE * NSA_KV_HEADS
    row = lambda w: pl.BlockSpec((tq, w), lambda b, s, pt: (b, 0))
    state = pl.BlockSpec((1, win, KV_DIM), lambda b, s, pt: (b, 0, 0))
    full3 = lambda shp: pl.BlockSpec(shp, lambda b, s, pt: (0, 0, 0))

    def page(r, phase):
        if phase == 0:
            return pl.BlockSpec((page_rows, HEAD_DIM),
                                lambda b, s, pt: (page0 + pt[b, jnp.minimum(s, ns - 1) * pp + r], 0))
        return pl.BlockSpec((page_rows, HEAD_DIM),
                            lambda b, s, pt: (page0 + pt[b, jnp.maximum(s - ns, 0) * pp + r], 0))

    in_specs = ([row(NSA_DIM), row(NSA_DIM), row(U_TAIL)] + [row(KV_DIM)] * 6 + [state, state,
                full3((2, CMP_BLOCK, HEAD_DIM)), full3((2, HEAD_DIM, HEAD_DIM))]
                + [page(r, 0) for r in range(pp)] * 2 + [page(r, 1) for r in range(pp)] * 2)
    bf16, f32 = jnp.bfloat16, jnp.float32
    grid_spec = pltpu.PrefetchScalarGridSpec(
        num_scalar_prefetch=1,
        grid=(bsz, 2 * ns),
        in_specs=in_specs,
        out_specs=[row(NSA_DIM), state, state],
        scratch_shapes=[pltpu.VMEM((ncp * NSA_KV_HEADS, HEAD_DIM), f32)] * 2 + [
            pltpu.VMEM((NSA_KV_HEADS * tq, ncp), f32), pltpu.VMEM((NSA_KV_HEADS * rows, HEAD_DIM), f32),
            pltpu.VMEM((NSA_KV_HEADS * rows, 1), f32), pltpu.VMEM((NSA_KV_HEADS * rows, 1), f32),
            pltpu.VMEM((NSA_KV_HEADS * rows, HEAD_DIM), f32)])
    return pl.pallas_call(
        functools.partial(_nsa_sample_kernel, tq=tq, past_len=past_len, ns=ns),
        grid_spec=grid_spec,
        out_shape=[jax.ShapeDtypeStruct((bsz * tq, NSA_DIM), f32),
                   jax.ShapeDtypeStruct((bsz, win, KV_DIM), f32), jax.ShapeDtypeStruct((bsz, win, KV_DIM), f32)],
        compiler_params=_cparams(("parallel", "arbitrary")),
        name="nsa_sample",
    )(page_table, qn, qr, tail, kc, vc, ks, vs, kw, vw, swa_k, swa_v, pe, cw,
      *([cache_ck] * pp + [cache_cv] * pp + [cache_sk] * pp + [cache_sv] * pp))


def _rope_tables(pos, bsz):
    half = HEAD_DIM // 2
    inv = ROPE_THETA ** (-jnp.arange(half, dtype=jnp.float32) / half)
    ang = pos.astype(jnp.float32)[:, None] * inv[None, :]
    cos = jnp.cos(ang)
    sin = jnp.sin(ang)
    cos2 = jnp.concatenate([cos, cos], axis=-1)
    sin2 = jnp.concatenate([-sin, sin], axis=-1)
    return jnp.tile(cos2, (bsz, 1)), jnp.tile(sin2, (bsz, 1))


def _trunk_layer(x, p_emb, pos, lp, st, win_buf, layer):
    bsz, t, _ = x.shape
    m = bsz * t
    bf16, f32 = jnp.bfloat16, jnp.float32
    prompt = st is None
    act_dtype = bf16 if prompt else f32
    x2 = x.reshape(m, D_MODEL)
    h = _rmsnorm_bf16(x2, lp['attn_norm_g'])
    u_a = _matmul_w(h, lp['w_in'], layer, n_cols=U_A)
    u_b = _matmul(h, lp['w_in_b'])
    tail = _matmul(h, lp['w_in_tail'], tn=U_TAIL)
    u3 = u_a.reshape(bsz, t, U_A)
    tail3 = tail.reshape(bsz, t, U_TAIL)
    if prompt:
        buf_a = jnp.zeros((bsz, CONV_A_HALO, CONV_A_DIM), f32)
        buf_ssm = jnp.zeros((bsz, SSM_HALO, SSM_CONV_DIM), f32)
        h0 = jnp.zeros((bsz, SSM_DIM, SSM_STATE), f32)
        buf_f = None
    else:
        buf_a = jnp.pad(st['conv_a'], ((0, 0), (CONV_A_HALO - (CONV_A_WIDTH - 1), 0), (0, 0)))
        buf_ssm = jnp.pad(st['ssm_conv'], ((0, 0), (SSM_HALO - (SSM_CONV - 1), 0), (0, 0)))
        h0 = st['ssm'].reshape(bsz, SSM_DIM, SSM_STATE)
        buf_f = st['ffn_conv']

    a_out, nb_a = _conformer(u3, buf_a, lp['conv_a_w'], lp['conv_a_b'], lp['conv_a_ln_g'], lp['conv_a_ln_b'], act_dtype)
    new_a = nb_a[:, CONV_A_HALO - (CONV_A_WIDTH - 1):]

    if t % SSM_CHUNK:
        padt = ((0, 0), (0, SSM_CHUNK - t), (0, 0))
        u3s, tail3s, valid = jnp.pad(u3, padt), jnp.pad(tail3, padt), t
    else:
        u3s, tail3s, valid = u3, tail3, SSM_CHUNK
    pad_h = lambda v: jnp.pad(v.reshape(1, SSM_HEADS), ((0, 0), (0, LANES - SSM_HEADS)))
    b_out, h_new, nb_s = _ssd(u3s, tail3s, buf_ssm, h0, lp['ssm_conv_w'], lp['ssm_conv_b'].reshape(1, -1),
                              pad_h(lp['ssm_dt_bias']), pad_h(lp['ssm_a_log']),
                              jnp.repeat(lp['ssm_d'], SSM_HEADDIM).reshape(1, SSM_DIM),
                              lp['ssm_norm_g'].reshape(1, SSM_DIM), valid, bf16 if prompt else f32)
    b_out = b_out[:, :t]
    new_h = h_new.reshape(bsz, SSM_HEADS, SSM_HEADDIM, SSM_STATE)
    new_ssm_conv = nb_s[:, SSM_HALO - (SSM_CONV - 1):]

    cos, sin = _rope_tables(pos, bsz)
    qn, qr, kc, vc, ks, vs, kw, vw = _nsa_prep(u_b, cos, sin, lp['nsa_q_norm_g'], lp['nsa_k_norm_g'],
                                               bf16 if prompt else f32)
    kv4 = lambda v: v.reshape(bsz, -1, NSA_KV_HEADS, HEAD_DIM)
    if prompt:
        c_out = _nsa_attn_prompt(qn, qr, tail, kc, vc, ks, vs, kw, vw, lp['nsa_cmp_pe'], lp['nsa_cmp_w'], bsz, t)
        zpad = jnp.zeros((bsz, win_buf, NSA_KV_HEADS, HEAD_DIM), f32)
        kw_new = jnp.concatenate([zpad, kv4(kw)], axis=1)[:, t:]
        vw_new = jnp.concatenate([zpad, kv4(vw)], axis=1)[:, t:]
    else:
        flat3 = lambda v: v.reshape(-1, v.shape[-3], KV_DIM)
        rows2 = lambda v: v.reshape(-1, HEAD_DIM)
        n_phys = st['cmp_k'].shape[1]
        c_out, kw3, vw3 = _nsa_sample(qn, qr, tail, kc, vc, ks, vs, kw, vw, flat3(st['swa_k']), flat3(st['swa_v']),
                                      rows2(st['cmp_k']), rows2(st['cmp_v']), rows2(st['sel_k']), rows2(st['sel_v']),
                                      st['page_table'], lp['nsa_cmp_pe'], lp['nsa_cmp_w'], bsz, t, layer * n_phys)
        kw_new, vw_new = kv4(kw3), kv4(vw3)
    nsa_state = (kv4(kc), kv4(vc), kv4(ks), kv4(vs), kw_new, vw_new)

    mix = jnp.concatenate([a_out.reshape(m, -1).astype(bf16), b_out.reshape(m, -1).astype(bf16), c_out.astype(bf16)],
                          axis=-1)
    x2 = _matmul_w(mix, lp['w_out'], layer, residual=x2)
    h2 = _rmsnorm_bf16(x2, lp['ffn_norm_g'])
    if prompt:
        act, new_f = _ffn_up_act(h2, lp['w_up'], layer, lp['ffn_conv_w'], lp['ffn_conv_b'], bsz, t)
    else:
        gu = _matmul_w(h2, lp['w_up'], layer).reshape(bsz, t, 2 * D_FF)
        act, new_f = _ffn_act(gu, buf_f, lp['ffn_conv_w'], lp['ffn_conv_b'], act_dtype, nseq=bsz, tc=D_FF_HALF)
        act = act.reshape(m, D_FF).astype(bf16)
    x2 = _matmul_w(act, lp['w_down'], layer, residual=x2, tm=512, tk=D_FF_HALF, kblk=0)
    x2 = _matmul_w(act, lp['w_down'], layer, residual=x2, tm=512, tk=D_FF_HALF, kblk=1)
    h3 = _rmsnorm_bf16(x2, lp['ple_norm_g'])
    x2 = _ple(h3, lp['w_ple_gate'], p_emb.reshape(m, PLE_DIM), lp['w_ple_proj'], x2, layer)
    return x2.reshape(bsz, t, D_MODEL), nsa_state + (new_h, new_ssm_conv, new_a, new_f)


def _prep_weights(w_in):
    bf16 = jnp.bfloat16
    w_in_b = w_in[:, ORIG_Q:ORIG_GATES].astype(bf16)
    w_in_tail = jnp.concatenate([w_in[:, ORIG_DT:ORIG_Q], w_in[:, ORIG_GATES:],
                                 jnp.zeros((D_MODEL, U_TAIL - SSM_HEADS - 3 * NSA_HEADS), w_in.dtype)], axis=1).astype(bf16)
    return dict(w_in_b=w_in_b, w_in_tail=w_in_tail)


def kernel(x_prompt, x_sample, cache_cmp_k, cache_cmp_v, cache_sel_k, cache_sel_v, state_swa_k, state_swa_v,
           state_ssm, state_ssm_conv, state_conv_a, state_ffn_conv, page_table, p_prompt, p_sample,
           attn_norm_g, w_in, conv_a_w, conv_a_b, conv_a_ln_g, conv_a_ln_b, ssm_conv_w, ssm_conv_b,
           ssm_dt_bias, ssm_a_log, ssm_d, ssm_norm_g, nsa_q_norm_g, nsa_k_norm_g, nsa_cmp_pe, nsa_cmp_w,
           w_out, ffn_norm_g, w_up, ffn_conv_w, ffn_conv_b, w_down, ple_norm_g, w_ple_gate, w_ple_proj):
    past_len = page_table.shape[1] * PAGE_SIZE
    win_buf = state_swa_k.shape[2]
    pos_p = jnp.arange(x_prompt.shape[1], dtype=jnp.int32)
    pos_s = past_len + jnp.arange(x_sample.shape[1], dtype=jnp.int32)
    y_p, y_s = x_prompt, x_sample
    states_p, states_s = [], []
    for i in range(DEPTH):
        lp = {'attn_norm_g': attn_norm_g[i], 'conv_a_w': conv_a_w[i], 'conv_a_b': conv_a_b[i],
              'conv_a_ln_g': conv_a_ln_g[i], 'conv_a_ln_b': conv_a_ln_b[i], 'ssm_conv_w': ssm_conv_w[i],
              'ssm_conv_b': ssm_conv_b[i], 'ssm_dt_bias': ssm_dt_bias[i], 'ssm_a_log': ssm_a_log[i],
              'ssm_d': ssm_d[i], 'ssm_norm_g': ssm_norm_g[i], 'nsa_q_norm_g': nsa_q_norm_g[i],
              'nsa_k_norm_g': nsa_k_norm_g[i], 'nsa_cmp_pe': nsa_cmp_pe[i], 'nsa_cmp_w': nsa_cmp_w[i],
              'ffn_norm_g': ffn_norm_g[i], 'ple_norm_g': ple_norm_g[i],
              'w_in': w_in, 'w_out': w_out, 'w_up': w_up, 'w_down': w_down,
              'w_ple_gate': w_ple_gate, 'w_ple_proj': w_ple_proj,
              'ffn_conv_w': ffn_conv_w[i], 'ffn_conv_b': ffn_conv_b[i].reshape(1, D_FF)}
        lp.update(_prep_weights(w_in[i]))
        st = {'cmp_k': cache_cmp_k, 'cmp_v': cache_cmp_v, 'sel_k': cache_sel_k, 'sel_v': cache_sel_v,
              'swa_k': state_swa_k[i], 'swa_v': state_swa_v[i], 'ssm': state_ssm[i], 'ssm_conv': state_ssm_conv[i],
              'conv_a': state_conv_a[i], 'ffn_conv': state_ffn_conv[i], 'page_table': page_table}
        y_p, sp = _trunk_layer(y_p, p_prompt[i], pos_p, lp, None, win_buf, i)
        y_s, ss = _trunk_layer(y_s, p_sample[i], pos_s, lp, st, win_buf, i)
        states_p.append(sp)
        states_s.append(ss)
    (ck_p, cv_p, sk_p, sv_p, wk_p, wv_p, ssm_p, sc_p, ca_p, fc_p) = [jnp.stack(z) for z in zip(*states_p)]
    (ck_s, cv_s, sk_s, sv_s, wk_s, wv_s, ssm_s, sc_s, ca_s, fc_s) = [jnp.stack(z) for z in zip(*states_s)]
    return (y_p, y_s, ck_p, ck_s, cv_p, cv_s, sk_p, sk_s, sv_p, sv_s, wk_p, wk_s, wv_p, wv_s,
            ssm_p, ssm_s, sc_p, sc_s, ca_p, ca_s, fc_p, fc_s)
```

```python
import functools

import jax
import jax.numpy as jnp
from jax import lax
from jax.experimental import pallas as pl
from jax.experimental.pallas import tpu as pltpu

D_MODEL = 4096
DEPTH = 2
PAGE_SIZE = 128
PLE_DIM = 256
CONV_A_DIM = 1024
CONV_A_WIDTH = 31
SSM_DIM = 1024
SSM_HEADDIM = 64
SSM_HEADS = SSM_DIM // SSM_HEADDIM
SSM_GROUPS = 4
SSM_STATE = 128
SSM_CONV = 4
SSM_CHUNK = 128
SSM_CONV_DIM = SSM_DIM + 2 * SSM_GROUPS * SSM_STATE
NSA_HEADS = 16
NSA_KV_HEADS = 4
GQA = NSA_HEADS // NSA_KV_HEADS
HEAD_DIM = 128
NSA_DIM = NSA_HEADS * HEAD_DIM
KV_DIM = NSA_KV_HEADS * HEAD_DIM
CMP_BLOCK = 32
SEL_BLOCK = 64
CMP_PER_SEL = SEL_BLOCK // CMP_BLOCK
SEL_TOPK = 16
WINDOW = 512
SWA_QBLOCK = 128
ROPE_THETA = 10000.0
D_FF = 11008
FFN_CONV = 3
EPS = 1e-6
NEG = -1e30
FORCE = 1e9
ATTN_SCALE = HEAD_DIM ** -0.5

LANES = 128
SUBLANES = 8

OFF_CONV = 0
OFF_Z = OFF_CONV + 2 * CONV_A_DIM
OFF_XBC = OFF_Z + SSM_DIM
U_A = OFF_XBC + SSM_CONV_DIM
OFF_Q = 0
OFF_KV = OFF_Q + NSA_DIM
U_B = OFF_KV + 6 * KV_DIM
U_TAIL = LANES
TAIL_GATES = SSM_HEADS
ORIG_DT = U_A
ORIG_Q = ORIG_DT + SSM_HEADS
ORIG_GATES = ORIG_Q + U_B

D_FF_HALF = D_FF // 2
FFN_TC = 256
ROW_CHUNK = 256
CONV_A_HALO = 32
CONV_A_ROWS = 32
SSM_HALO = SUBLANES
NSA_TQ = 256
NSA_KC = 1024
VMEM_LIMIT = 54 * 1024 * 1024

_NT = (((1,), (1,)), ((), ()))


def _cparams(sem):
    return pltpu.CompilerParams(dimension_semantics=sem, vmem_limit_bytes=VMEM_LIMIT)


def _dot(a, b):
    return jnp.dot(a, b, preferred_element_type=jnp.float32)


def _dot_nt(a, b):
    return lax.dot_general(a, b, _NT, preferred_element_type=jnp.float32)


def _dot_f32(a, b):
    return jnp.dot(a, b, preferred_element_type=jnp.float32, precision=lax.Precision.HIGHEST)


def _bf(x):
    return x.astype(jnp.bfloat16)


def _sigmoid(x):
    return jax.nn.sigmoid(x)


def _rmsnorm_kernel(x_ref, g_ref, o_ref):
    x = x_ref[...]
    ms = jnp.mean(x * x, axis=-1, keepdims=True)
    o_ref[...] = (x * lax.rsqrt(ms + EPS) * g_ref[...]).astype(o_ref.dtype)


def _rmsnorm_bf16(x, g):
    m, d = x.shape
    tm = min(m, 256)
    return pl.pallas_call(
        _rmsnorm_kernel,
        grid=(m // tm,),
        in_specs=[pl.BlockSpec((tm, d), lambda i: (i, 0)), pl.BlockSpec((1, d), lambda i: (0, 0))],
        out_specs=pl.BlockSpec((tm, d), lambda i: (i, 0)),
        out_shape=jax.ShapeDtypeStruct((m, d), jnp.bfloat16),
        compiler_params=_cparams(("parallel",)),
        name="rmsnorm",
    )(x, g.reshape(1, d))


def _mm_kernel(a_ref, b_ref, *rest, nk, has_res):
    if has_res:
        r_ref, o_ref, acc_ref = rest
    else:
        o_ref, acc_ref = rest
    k = pl.program_id(2)
    part = _dot(a_ref[...], b_ref[...])

    if nk == 1:
        o_ref[...] = part + r_ref[...] if has_res else part
        return

    @pl.when(k == 0)
    def _():
        acc_ref[...] = part

    @pl.when(k > 0)
    def _():
        acc_ref[...] += part

    @pl.when(k == nk - 1)
    def _():
        o_ref[...] = acc_ref[...] + r_ref[...] if has_res else acc_ref[...]


def _matmul(a, b, residual=None, *, tm=1024, tn=512, tk=None):
    m, kdim = a.shape
    _, n = b.shape
    tm = min(tm, m)
    tn = min(tn, n)
    tk = kdim if tk is None else tk
    nk = kdim // tk
    in_specs = [pl.BlockSpec((tm, tk), lambda i, j, k: (i, k)), pl.BlockSpec((tk, tn), lambda i, j, k: (k, j))]
    args = [a, b]
    if residual is not None:
        in_specs.append(pl.BlockSpec((tm, tn), lambda i, j, k: (i, j)))
        args.append(residual)
    return pl.pallas_call(
        functools.partial(_mm_kernel, nk=nk, has_res=residual is not None),
        grid=(m // tm, n // tn, nk),
        in_specs=in_specs,
        out_specs=pl.BlockSpec((tm, tn), lambda i, j, k: (i, j)),
        out_shape=jax.ShapeDtypeStruct((m, n), jnp.float32),
        scratch_shapes=[pltpu.VMEM((tm, tn) if nk > 1 else (SUBLANES, LANES), jnp.float32)],
        compiler_params=_cparams(("parallel", "parallel", "arbitrary")),
        name="matmul",
    )(*args)


def _mmw_kernel(a_ref, w_ref, *rest, has_res):
    if has_res:
        r_ref, o_ref, wb_ref = rest
    else:
        o_ref, wb_ref = rest

    @pl.when(pl.program_id(1) == 0)
    def _():
        wb_ref[...] = _bf(w_ref[...])

    part = _dot(a_ref[...], wb_ref[...])
    o_ref[...] = part + r_ref[...] if has_res else part


def _matmul_w(a, w, layer, residual=None, *, n_cols=None, tm=1024, tn=512, tk=None, kblk=0):
    m = a.shape[0]
    tk = w.shape[1] if tk is None else tk
    n = w.shape[2] if n_cols is None else n_cols
    tm = min(tm, m)
    in_specs = [pl.BlockSpec((tm, tk), lambda j, i: (i, kblk)),
                pl.BlockSpec((None, tk, tn), lambda j, i: (layer, kblk, j))]
    args = [a, w]
    if residual is not None:
        in_specs.append(pl.BlockSpec((tm, tn), lambda j, i: (i, j)))
        args.append(residual)
    return pl.pallas_call(
        functools.partial(_mmw_kernel, has_res=residual is not None),
        grid=(n // tn, m // tm),
        in_specs=in_specs,
        out_specs=pl.BlockSpec((tm, tn), lambda j, i: (i, j)),
        out_shape=jax.ShapeDtypeStruct((m, n), jnp.float32),
        scratch_shapes=[pltpu.VMEM((tk, tn), jnp.bfloat16)],
        compiler_params=_cparams(("parallel", "arbitrary")),
        name="matmul_w",
    )(*args)


def _ple_kernel(h_ref, wg_ref, p_ref, wp_ref, x_ref, o_ref, wgb_ref, wpb_ref):
    @pl.when(pl.program_id(1) == 0)
    def _():
        wgb_ref[...] = _bf(wg_ref[...])
        wpb_ref[...] = _bf(wp_ref[...])

    gate = _sigmoid(_dot(h_ref[...], wgb_ref[...]))
    proj = _dot(_bf(p_ref[...]), wpb_ref[...])
    o_ref[...] = x_ref[...] + proj * gate


def _ple(h, wg, p, wp, x, layer, *, tm=1024, tn=512):
    m, d = h.shape
    n = wg.shape[2]
    tm = min(tm, m)
    return pl.pallas_call(
        _ple_kernel,
        grid=(n // tn, m // tm),
        in_specs=[pl.BlockSpec((tm, d), lambda j, i: (i, 0)), pl.BlockSpec((None, d, tn), lambda j, i: (layer, 0, j)),
                  pl.BlockSpec((tm, PLE_DIM), lambda j, i: (i, 0)),
                  pl.BlockSpec((None, PLE_DIM, tn), lambda j, i: (layer, 0, j)),
                  pl.BlockSpec((tm, tn), lambda j, i: (i, j))],
        out_specs=pl.BlockSpec((tm, tn), lambda j, i: (i, j)),
        out_shape=jax.ShapeDtypeStruct((m, n), jnp.float32),
        scratch_shapes=[pltpu.VMEM((d, tn), jnp.bfloat16), pltpu.VMEM((PLE_DIM, tn), jnp.bfloat16)],
        compiler_params=_cparams(("parallel", "arbitrary")),
        name="ple",
    )(h, wg, p, wp, x)


def _ffn_act_kernel(g_ref, u_ref, buf_ref, w_ref, b_ref, act_ref, nb_ref, *, nseq, t, rc):
    w = w_ref[...]
    bias = b_ref[...]
    for s in range(nseq):
        buf = buf_ref[s]

        def body(ci, carry, s=s, buf=buf):
            r0 = pl.multiple_of(ci * rc, rc)
            cur = g_ref[s, pl.ds(r0, rc), :]
            p0 = pl.multiple_of(jnp.maximum(r0 - SUBLANES, 0), SUBLANES)
            prev = g_ref[s, pl.ds(p0, SUBLANES), :]
            first = ci == 0
            hm1 = jnp.where(first, buf[1:2], prev[SUBLANES - 1:SUBLANES])
            hm2 = jnp.where(first, buf[0:1], prev[SUBLANES - 2:SUBLANES - 1])
            row = lax.broadcasted_iota(jnp.int32, cur.shape, 0)
            g1 = jnp.where(row == 0, hm1, pltpu.roll(cur, 1, axis=0))
            g2 = jnp.where(row == 0, hm2, jnp.where(row == 1, hm1, pltpu.roll(cur, 2, axis=0)))
            c = w[0:1] * g2 + w[1:2] * g1 + w[2:3] * cur + bias
            act_ref[s, pl.ds(r0, rc), :] = (c * _sigmoid(c) * u_ref[s, pl.ds(r0, rc), :]).astype(act_ref.dtype)
            return carry

        lax.fori_loop(0, t // rc, body, 0)
        nb_ref[s] = g_ref[s, t - 2:t, :]


def _ffn_act(gu, buf, w, b, out_dtype, *, nseq, tc):
    bsz, t, _ = gu.shape
    nc = D_FF // tc
    rc = min(t, ROW_CHUNK)
    return pl.pallas_call(
        functools.partial(_ffn_act_kernel, nseq=nseq, t=t, rc=rc),
        grid=(bsz // nseq, nc),
        in_specs=[pl.BlockSpec((nseq, t, tc), lambda bi, j: (bi, 0, j)),
                  pl.BlockSpec((nseq, t, tc), lambda bi, j: (bi, 0, j + nc)),
                  pl.BlockSpec((nseq, FFN_CONV - 1, tc), lambda bi, j: (bi, 0, j)),
                  pl.BlockSpec((FFN_CONV, tc), lambda bi, j: (0, j)),
                  pl.BlockSpec((1, tc), lambda bi, j: (0, j))],
        out_specs=[pl.BlockSpec((nseq, t, tc), lambda bi, j: (bi, 0, j)),
                   pl.BlockSpec((nseq, FFN_CONV - 1, tc), lambda bi, j: (bi, 0, j))],
        out_shape=[jax.ShapeDtypeStruct((bsz, t, D_FF), out_dtype),
                   jax.ShapeDtypeStruct((bsz, FFN_CONV - 1, D_FF), jnp.float32)],
        compiler_params=_cparams(("parallel", "parallel")),
        name="ffn_act",
    )(gu, gu, buf, w, b)


def _ffn_up_kernel(a_ref, wg_ref, wu_ref, cw_ref, cb_ref, act_ref, nf_ref, wb_ref, gu_ref, *, tm, tn, rc, tiles_per_seq):
    i = pl.program_id(1)

    @pl.when(i == 0)
    def _():
        wb_ref[:, 0:tn] = _bf(wg_ref[...])
        wb_ref[:, tn:2 * tn] = _bf(wu_ref[...])

    gu_ref[SUBLANES:SUBLANES + tm, :] = _dot(a_ref[...], wb_ref[...])
    seq_start = (i % tiles_per_seq) == 0
    w = cw_ref[...]
    bias = cb_ref[...]

    def body(ci, carry):
        r0 = pl.multiple_of(ci * rc, rc)
        cur = gu_ref[pl.ds(SUBLANES + r0, rc), 0:tn]
        prev = gu_ref[pl.ds(r0, SUBLANES), 0:tn]
        fresh = seq_start & (ci == 0)
        hm1 = jnp.where(fresh, 0.0, prev[SUBLANES - 1:SUBLANES])
        hm2 = jnp.where(fresh, 0.0, prev[SUBLANES - 2:SUBLANES - 1])
        row = lax.broadcasted_iota(jnp.int32, cur.shape, 0)
        g1 = jnp.where(row == 0, hm1, pltpu.roll(cur, 1, axis=0))
        g2 = jnp.where(row == 0, hm2, jnp.where(row == 1, hm1, pltpu.roll(cur, 2, axis=0)))
        c = w[0:1] * g2 + w[1:2] * g1 + w[2:3] * cur + bias
        act_ref[pl.ds(r0, rc), :] = (c * _sigmoid(c) * gu_ref[pl.ds(SUBLANES + r0, rc), tn:2 * tn]).astype(act_ref.dtype)
        return carry

    lax.fori_loop(0, tm // rc, body, 0)
    last = gu_ref[tm:tm + SUBLANES, 0:tn]
    nf_ref[0] = last[SUBLANES - (FFN_CONV - 1):SUBLANES]
    gu_ref[0:SUBLANES, 0:tn] = last


def _ffn_up_act(h, w_up, layer, cw, cb, bsz, t, *, tm=1024, tn=FFN_TC):
    m, d = h.shape
    nc = D_FF // tn
    tiles_per_seq = t // tm
    return pl.pallas_call(
        functools.partial(_ffn_up_kernel, tm=tm, tn=tn, rc=ROW_CHUNK, tiles_per_seq=tiles_per_seq),
        grid=(nc, m // tm),
        in_specs=[pl.BlockSpec((tm, d), lambda j, i: (i, 0)),
                  pl.BlockSpec((None, d, tn), lambda j, i: (layer, 0, j)),
                  pl.BlockSpec((None, d, tn), lambda j, i: (layer, 0, j + nc)),
                  pl.BlockSpec((FFN_CONV, tn), lambda j, i: (0, j)),
                  pl.BlockSpec((1, tn), lambda j, i: (0, j))],
        out_specs=[pl.BlockSpec((tm, tn), lambda j, i: (i, j)),
                   pl.BlockSpec((1, FFN_CONV - 1, tn), lambda j, i: (i // tiles_per_seq, 0, j))],
        out_shape=[jax.ShapeDtypeStruct((m, D_FF), jnp.bfloat16),
                   jax.ShapeDtypeStruct((bsz, FFN_CONV - 1, D_FF), jnp.float32)],
        scratch_shapes=[pltpu.VMEM((d, 2 * tn), jnp.bfloat16), pltpu.VMEM((SUBLANES + tm, 2 * tn), jnp.float32)],
        compiler_params=_cparams(("parallel", "arbitrary")),
        name="ffn_up_act",
    )(h, w_up, w_up, cw, cb)


def _conformer_kernel(u_ref, buf_ref, w_ref, b_ref, lg_ref, lb_ref, o_ref, nb_ref, ext_ref, *, tt, rc):
    ti = pl.program_id(1)

    @pl.when(ti == 0)
    def _():
        ext_ref[0:CONV_A_HALO, :] = buf_ref[0]

    x = u_ref[0]
    ext_ref[CONV_A_HALO:CONV_A_HALO + tt, :] = x[:, :CONV_A_DIM] * _sigmoid(x[:, CONV_A_DIM:])
    bias = b_ref[...]
    lg = lg_ref[...]
    lb = lb_ref[...]
    first_tap = CONV_A_HALO - (CONV_A_WIDTH - 1)
    for r in range(tt // rc):
        r0 = r * rc
        acc = jnp.zeros((rc, CONV_A_DIM), jnp.float32) + bias
        for k in range(CONV_A_WIDTH):
            acc = acc + w_ref[k:k + 1, :] * ext_ref[r0 + first_tap + k:r0 + first_tap + k + rc, :]
        mu = jnp.mean(acc, axis=-1, keepdims=True)
        d = acc - mu
        var = jnp.mean(d * d, axis=-1, keepdims=True)
        y = d * lax.rsqrt(var + EPS) * lg + lb
        o_ref[0, r0:r0 + rc, :] = (y * _sigmoid(y)).astype(o_ref.dtype)
    carry = ext_ref[tt:tt + CONV_A_HALO, :]
    nb_ref[0] = carry
    ext_ref[0:CONV_A_HALO, :] = carry


def _conformer(u3, buf, w, b, lg, lb, out_dtype):
    bsz, t, _ = u3.shape
    tt = min(t, ROW_CHUNK)
    rc = min(tt, CONV_A_ROWS)
    vec = lambda: pl.BlockSpec((1, CONV_A_DIM), lambda bi, ti: (0, 0))
    return pl.pallas_call(
        functools.partial(_conformer_kernel, tt=tt, rc=rc),
        grid=(bsz, t // tt),
        in_specs=[pl.BlockSpec((1, tt, 2 * CONV_A_DIM), lambda bi, ti: (bi, ti, OFF_CONV // (2 * CONV_A_DIM))),
                  pl.BlockSpec((1, CONV_A_HALO, CONV_A_DIM), lambda bi, ti: (bi, 0, 0)),
                  pl.BlockSpec((CONV_A_WIDTH, CONV_A_DIM), lambda bi, ti: (0, 0)),
                  vec(), vec(), vec()],
        out_specs=[pl.BlockSpec((1, tt, CONV_A_DIM), lambda bi, ti: (bi, ti, 0)),
                   pl.BlockSpec((1, CONV_A_HALO, CONV_A_DIM), lambda bi, ti: (bi, 0, 0))],
        out_shape=[jax.ShapeDtypeStruct((bsz, t, CONV_A_DIM), out_dtype),
                   jax.ShapeDtypeStruct((bsz, CONV_A_HALO, CONV_A_DIM), jnp.float32)],
        scratch_shapes=[pltpu.VMEM((CONV_A_HALO + tt, CONV_A_DIM), jnp.float32)],
        compiler_params=_cparams(("parallel", "arbitrary")),
        name="conformer",
    )(u3, buf, w, b.reshape(1, -1), lg.reshape(1, -1), lb.reshape(1, -1))


def _ssd_kernel(z_ref, xbc_lo_ref, xbc_hi_ref, dt_ref, cbuf_ref, h0_ref, cw_ref, cb_ref, dtb_ref, alog_ref, dskip_ref,
                ng_ref, y_ref, hout_ref, cbout_ref, ext_ref, h_ref, ys_ref, *, valid):
    f32 = jnp.float32
    ln = SSM_CHUNK
    ci = pl.program_id(1)

    @pl.when(ci == 0)
    def _():
        ext_ref[0:SSM_HALO, :] = cbuf_ref[0]
        h_ref[...] = h0_ref[0]

    ext_ref[SSM_HALO:SSM_HALO + ln, 0:SSM_CONV_DIM // 2] = xbc_lo_ref[0]
    ext_ref[SSM_HALO:SSM_HALO + ln, SSM_CONV_DIM // 2:SSM_CONV_DIM] = xbc_hi_ref[0]
    conv = jnp.zeros((ln, SSM_CONV_DIM), f32) + cb_ref[...]
    for k in range(SSM_CONV):
        s0 = SSM_HALO - (SSM_CONV - 1) + k
        conv = conv + cw_ref[k:k + 1, :] * ext_ref[s0:s0 + ln, :]
    c = conv * _sigmoid(conv)
    gn = SSM_GROUPS * SSM_STATE

    row = lax.broadcasted_iota(jnp.int32, (ln, LANES), 0)
    lane = lax.broadcasted_iota(jnp.int32, (ln, LANES), 1)
    xr = dt_ref[0] + dtb_ref[...]
    dt = jnp.maximum(xr, 0.0) + jnp.log1p(jnp.exp(-jnp.abs(xr)))
    if valid < ln:
        dt = jnp.where(row < valid, dt, 0.0)
    la = dt * (-jnp.exp(alog_ref[...]))
    tril = jnp.where(row >= lane, 1.0, 0.0).astype(f32)
    acum = _dot_f32(tril, la)
    acum_t = acum.T
    alast = acum[ln - 1:ln, :]
    causal = row >= lane
    lane_lo = lane < SSM_HEADDIM
    pair_w = 2 * SSM_HEADDIM

    cb_cache = {}
    for j in range(SSM_HEADS // 2):
        h0i, h1i = 2 * j, 2 * j + 1
        g = h0i // (SSM_HEADS // SSM_GROUPS)
        bm = _bf(c[:, SSM_DIM + g * SSM_STATE:SSM_DIM + (g + 1) * SSM_STATE])
        cm = _bf(c[:, SSM_DIM + gn + g * SSM_STATE:SSM_DIM + gn + (g + 1) * SSM_STATE])
        if g not in cb_cache:
            cb_cache[g] = _dot_nt(cm, bm)
        cbg = cb_cache[g]
        a0, a1 = acum[:, h0i:h0i + 1], acum[:, h1i:h1i + 1]
        dec0 = jnp.where(causal, jnp.exp(a0 - acum_t[h0i:h0i + 1, :]), 0.0)
        dec1 = jnp.where(causal, jnp.exp(a1 - acum_t[h1i:h1i + 1, :]), 0.0)
        sc = jnp.concatenate([_bf(cbg * dec0), _bf(cbg * dec1)], axis=1)
        xs = c[:, j * pair_w:(j + 1) * pair_w]
        xdt = xs * jnp.where(lane_lo, dt[:, h0i:h0i + 1], dt[:, h1i:h1i + 1])
        xblk = jnp.concatenate([_bf(jnp.where(lane_lo, xdt, 0.0)), _bf(jnp.where(lane_lo, 0.0, xdt))], axis=0)
        y_diag = _dot(sc, xblk)
        hp = h_ref[j * pair_w:(j + 1) * pair_w, :]
        y_off = _dot_nt(cm, _bf(hp)) * jnp.where(lane_lo, jnp.exp(a0), jnp.exp(a1))
        ys_ref[:, j * pair_w:(j + 1) * pair_w] = y_diag + y_off + xs * dskip_ref[:, j * pair_w:(j + 1) * pair_w]
        al0, al1 = alast[:, h0i:h0i + 1], alast[:, h1i:h1i + 1]
        dend = jnp.where(lane_lo, jnp.exp(al0 - a0), jnp.exp(al1 - a1))
        s_new = _dot(_bf((xdt * dend).T), bm)
        h_ref[j * pair_w:(j + 1) * pair_w, :] = jnp.where(row < SSM_HEADDIM, jnp.exp(al0), jnp.exp(al1)) * hp + s_new

    z = z_ref[0]
    y = ys_ref[...] * (z * _sigmoid(z))
    gw = SSM_DIM // SSM_GROUPS
    for g in range(SSM_GROUPS):
        yg = y[:, g * gw:(g + 1) * gw]
        ms = jnp.mean(yg * yg, axis=-1, keepdims=True)
        y_ref[0, :, g * gw:(g + 1) * gw] = (yg * lax.rsqrt(ms + EPS) * ng_ref[:, g * gw:(g + 1) * gw]).astype(y_ref.dtype)

    tail = ext_ref[valid:valid + SSM_HALO, :]
    cbout_ref[0] = tail
    ext_ref[0:SSM_HALO, :] = tail
    hout_ref[0] = h_ref[...]


def _ssd(u3, tail3, cbuf, h0, cw, cb, dtb, alog, dskip, ng, valid, out_dtype):
    bsz, t, _ = u3.shape
    ln = SSM_CHUNK
    half = SSM_CONV_DIM // 2
    full = lambda shape: pl.BlockSpec(shape, lambda bi, ci: (0,) * len(shape))
    return pl.pallas_call(
        functools.partial(_ssd_kernel, valid=valid),
        grid=(bsz, t // ln),
        in_specs=[pl.BlockSpec((1, ln, SSM_DIM), lambda bi, ci: (bi, ci, OFF_Z // SSM_DIM)),
                  pl.BlockSpec((1, ln, half), lambda bi, ci: (bi, ci, OFF_XBC // half)),
                  pl.BlockSpec((1, ln, half), lambda bi, ci: (bi, ci, OFF_XBC // half + 1)),
                  pl.BlockSpec((1, ln, U_TAIL), lambda bi, ci: (bi, ci, 0)),
                  pl.BlockSpec((1, SSM_HALO, SSM_CONV_DIM), lambda bi, ci: (bi, 0, 0)),
                  pl.BlockSpec((1, SSM_DIM, SSM_STATE), lambda bi, ci: (bi, 0, 0)),
                  full((SSM_CONV, SSM_CONV_DIM)), full((1, SSM_CONV_DIM)), full((1, LANES)), full((1, LANES)),
                  full((1, SSM_DIM)), full((1, SSM_DIM))],
        out_specs=[pl.BlockSpec((1, ln, SSM_DIM), lambda bi, ci: (bi, ci, 0)),
                   pl.BlockSpec((1, SSM_DIM, SSM_STATE), lambda bi, ci: (bi, 0, 0)),
                   pl.BlockSpec((1, SSM_HALO, SSM_CONV_DIM), lambda bi, ci: (bi, 0, 0))],
        out_shape=[jax.ShapeDtypeStruct((bsz, t, SSM_DIM), out_dtype),
                   jax.ShapeDtypeStruct((bsz, SSM_DIM, SSM_STATE), jnp.float32),
                   jax.ShapeDtypeStruct((bsz, SSM_HALO, SSM_CONV_DIM), jnp.float32)],
        scratch_shapes=[pltpu.VMEM((SSM_HALO + ln, SSM_CONV_DIM), jnp.float32),
                        pltpu.VMEM((SSM_DIM, SSM_STATE), jnp.float32),
                        pltpu.VMEM((ln, SSM_DIM), jnp.float32)],
        compiler_params=_cparams(("parallel", "arbitrary")),
        name="ssd",
    )(u3, u3, u3, tail3, cbuf, h0, cw, cb, dtb, alog, dskip, ng)


def _nsa_prep_kernel(q_ref, kvc_ref, kvs_ref, kvw_ref, cos_ref, sin_ref, qg_ref, kg_ref,
                     qn_ref, qr_ref, kc_ref, vc_ref, ks_ref, vs_ref, kw_ref, vw_ref):
    cos = cos_ref[...]
    sin = sin_ref[...]

    def norm(x, g):
        return x * lax.rsqrt(jnp.mean(x * x, axis=-1, keepdims=True) + EPS) * g

    def rope(x):
        return x * cos + pltpu.roll(x, HEAD_DIM // 2, axis=1) * sin

    qg = qg_ref[...]
    for h in range(NSA_HEADS):
        sl = slice(h * HEAD_DIM, (h + 1) * HEAD_DIM)
        x = norm(q_ref[:, sl], qg)
        qn_ref[:, sl] = x.astype(qn_ref.dtype)
        qr_ref[:, sl] = rope(x).astype(qr_ref.dtype)
    for h in range(NSA_KV_HEADS):
        sl = slice(h * HEAD_DIM, (h + 1) * HEAD_DIM)
        sv = slice(KV_DIM + h * HEAD_DIM, KV_DIM + (h + 1) * HEAD_DIM)
        kc_ref[:, sl] = norm(kvc_ref[:, sl], kg_ref[0:1, :])
        vc_ref[:, sl] = kvc_ref[:, sv]
        ks_ref[:, sl] = rope(norm(kvs_ref[:, sl], kg_ref[1:2, :]))
        vs_ref[:, sl] = kvs_ref[:, sv]
        kw_ref[:, sl] = rope(norm(kvw_ref[:, sl], kg_ref[2:3, :]))
        vw_ref[:, sl] = kvw_ref[:, sv]


def _nsa_prep(u, cos, sin, qg, kg, q_dtype):
    m = u.shape[0]
    tt = min(m, ROW_CHUNK)
    row = lambda w, blk=0: pl.BlockSpec((tt, w), lambda i: (i, blk))
    kv = jax.ShapeDtypeStruct((m, KV_DIM), jnp.float32)
    qo = jax.ShapeDtypeStruct((m, NSA_DIM), q_dtype)
    kv0 = OFF_KV // (2 * KV_DIM)
    return pl.pallas_call(
        _nsa_prep_kernel,
        grid=(m // tt,),
        in_specs=[row(NSA_DIM, OFF_Q // NSA_DIM), row(2 * KV_DIM, kv0), row(2 * KV_DIM, kv0 + 1), row(2 * KV_DIM, kv0 + 2),
                  row(HEAD_DIM), row(HEAD_DIM),
                  pl.BlockSpec((1, HEAD_DIM), lambda i: (0, 0)), pl.BlockSpec((3, HEAD_DIM), lambda i: (0, 0))],
        out_specs=[row(NSA_DIM), row(NSA_DIM)] + [row(KV_DIM)] * 6,
        out_shape=[qo, qo] + [kv] * 6,
        compiler_params=_cparams(("parallel",)),
        name="nsa_prep",
    )(u, u, u, u, cos, sin, qg.reshape(1, HEAD_DIM), kg)


def _softmax_rows(s, divide=False):
    m = jnp.max(s, axis=-1, keepdims=True)
    e = jnp.exp(s - m)
    den = jnp.sum(e, axis=-1, keepdims=True)
    if divide:
        return e / den
    return e * (1.0 / den)


def _nsa_attn_kernel(qn_ref, qr_ref, tail_ref, kc_ref, vc_ref, ks_ref, vs_ref, kw_ref, vw_ref, pe_ref, cw_ref,
                     o_ref, kcmp_ref, kcmpp_ref, vcmp_ref, ksb_ref, vsb_ref, kwb_ref, vwb_ref,
                     m_ref, l_ref, acc_ref, *, t, tq):
    f32 = jnp.float32
    kvh = pl.program_id(1)
    qi = pl.program_id(2)
    ncb = t // CMP_BLOCK
    nsb = t // SEL_BLOCK
    rows = GQA * tq
    span = min(WINDOW + tq, t)
    cmp_shift = CMP_BLOCK.bit_length() - 1
    sel_shift = SEL_BLOCK.bit_length() - 1

    @pl.when(qi == 0)
    def _():
        r = lax.broadcasted_iota(jnp.int32, (ncb, t), 0)
        cblk = lax.broadcasted_iota(jnp.int32, (ncb, t), 1) >> cmp_shift
        avg_nat = jnp.where(cblk == r, 1.0 / CMP_BLOCK, 0.0).astype(f32)
        perm = jnp.where(r < nsb, 2 * r, 2 * (r - nsb) + 1)
        avg_perm = jnp.where(cblk == perm, 1.0 / CMP_BLOCK, 0.0).astype(f32)
        kc = kc_ref[...]
        pe_k = jnp.mean(pe_ref[0], axis=0, keepdims=True)
        pe_v = jnp.mean(pe_ref[1], axis=0, keepdims=True)
        wk = _bf(cw_ref[0])
        wv = _bf(cw_ref[1])
        kcmp_ref[...] = _bf(_dot(_bf(_dot_f32(avg_nat, kc) + pe_k), wk))
        kcmpp_ref[...] = _bf(_dot(_bf(_dot_f32(avg_perm, kc) + pe_k), wk))
        vcmp_ref[...] = _bf(_dot(_bf(_dot_f32(avg_nat, vc_ref[...]) + pe_v), wv))
        ksb_ref[...] = _bf(ks_ref[...])
        vsb_ref[...] = _bf(vs_ref[...])
        kwb_ref[...] = _bf(kw_ref[...])
        vwb_ref[...] = _bf(vw_ref[...])

    t0 = qi * tq

    def stack(ref):
        return jnp.concatenate([ref[:, g * HEAD_DIM:(g + 1) * HEAD_DIM] for g in range(GQA)], axis=0)

    qn = stack(qn_ref)

    tpos_c = t0 + (lax.broadcasted_iota(jnp.int32, (rows, ncb), 0) & (tq - 1))
    blk_end = (lax.broadcasted_iota(jnp.int32, (rows, ncb), 1) + 1) * CMP_BLOCK - 1
    s = jnp.where(blk_end <= tpos_c, _dot_nt(qn, kcmp_ref[...]) * ATTN_SCALE, NEG)
    anyvis = jnp.where(tpos_c[:, 0:1] >= CMP_BLOCK - 1, 1.0, 0.0).astype(f32)
    p = _softmax_rows(s) * anyvis
    o_cmp = _dot(_bf(p), vcmp_ref[...])

    rperm = lax.broadcasted_iota(jnp.int32, (ncb, rows), 0)
    blk_t = jnp.where(rperm < nsb, 2 * rperm, 2 * (rperm - nsb) + 1)
    tpos_t = t0 + (lax.broadcasted_iota(jnp.int32, (ncb, rows), 1) & (tq - 1))
    st = jnp.where((blk_t + 1) * CMP_BLOCK - 1 <= tpos_t, _dot_nt(kcmpp_ref[...], qn) * ATTN_SCALE, NEG)
    mt = jnp.max(st, axis=0, keepdims=True)
    et = jnp.exp(st - mt)
    pt = et / jnp.sum(et, axis=0, keepdims=True) * jnp.where(tpos_t[0:1, :] >= CMP_BLOCK - 1, 1.0, 0.0).astype(f32)
    psum = pt[:, 0:tq]
    for g in range(1, GQA):
        psum = psum + pt[:, g * tq:(g + 1) * tq]
    imp = psum[0:nsb, :] + psum[nsb:2 * nsb, :]
    jrow = lax.broadcasted_iota(jnp.int32, (nsb, tq), 0)
    qp = t0 + lax.broadcasted_iota(jnp.int32, (nsb, tq), 1)
    forced = (jrow == (qp >> sel_shift)) | (jrow == 0)
    imp = jnp.where(forced, FORCE, jnp.where(jrow * SEL_BLOCK > qp, NEG, imp))
    cnt = jnp.zeros((nsb, tq), f32)
    for i in range(nsb):
        ri = imp[i:i + 1, :]
        cnt = cnt + jnp.where((ri > imp) | ((ri == imp) & (jrow > i)), 1.0, 0.0)
    sel_t = jnp.where(cnt < SEL_TOPK, 1.0, 0.0).astype(f32)
    sel_pad = jnp.concatenate([sel_t, jnp.zeros((LANES - nsb, tq), f32)], axis=0) if nsb < LANES else sel_t
    sel_q = _bf(sel_pad.T)

    kc_sz = min(NSA_KC, t)
    m_ref[...] = jnp.full((rows, 1), NEG, f32)
    l_ref[...] = jnp.zeros((rows, 1), f32)
    acc_ref[...] = jnp.zeros((rows, HEAD_DIM), f32)
    qpos_s = t0 + lax.broadcasted_iota(jnp.int32, (tq, kc_sz), 0)

    def sel_body(ck, carry):
        k0 = pl.multiple_of(ck * kc_sz, kc_sz)
        kb = ksb_ref[pl.ds(k0, kc_sz), :]
        vb = vsb_ref[pl.ds(k0, kc_sz), :]
        kpos = k0 + lax.broadcasted_iota(jnp.int32, (tq, kc_sz), 1)
        eblk = (k0 + lax.broadcasted_iota(jnp.int32, (LANES, kc_sz), 1)) >> sel_shift
        expand = jnp.where(eblk == lax.broadcasted_iota(jnp.int32, (LANES, kc_sz), 0), 1.0, 0.0).astype(jnp.bfloat16)
        chosen = _dot(sel_q, expand)
        visible = (chosen > 0.5) & (kpos <= qpos_s)
        for g in range(GQA):
            rs = slice(g * tq, (g + 1) * tq)
            sc = jnp.where(visible, _dot_nt(qr_ref[:, g * HEAD_DIM:(g + 1) * HEAD_DIM], kb) * ATTN_SCALE, NEG)
            m_old = m_ref[rs, :]
            m_new = jnp.maximum(m_old, jnp.max(sc, axis=-1, keepdims=True))
            alpha = jnp.exp(m_old - m_new)
            pc = jnp.exp(sc - m_new)
            l_ref[rs, :] = alpha * l_ref[rs, :] + jnp.sum(pc, axis=-1, keepdims=True)
            acc_ref[rs, :] = alpha * acc_ref[rs, :] + _dot(_bf(pc), vb)
            m_ref[rs, :] = m_new
        return carry

    lax.fori_loop(0, (t0 + tq + kc_sz - 1) // kc_sz, sel_body, 0)

    ws = pl.multiple_of(jnp.maximum(t0 + tq - span, 0), SWA_QBLOCK)
    kb = kwb_ref[pl.ds(ws, span), :]
    vb = vwb_ref[pl.ds(ws, span), :]
    diff = (t0 + lax.broadcasted_iota(jnp.int32, (tq, span), 0)) - (ws + lax.broadcasted_iota(jnp.int32, (tq, span), 1))
    win_vis = (diff >= 0) & (diff <= WINDOW)
    sg = _sigmoid(tail_ref[...])
    glane = lax.broadcasted_iota(jnp.int32, (tq, U_TAIL), 1)
    for g in range(GQA):
        rs = slice(g * tq, (g + 1) * tq)
        hs = slice(g * HEAD_DIM, (g + 1) * HEAD_DIM)
        sw = jnp.where(win_vis, _dot_nt(qr_ref[:, hs], kb) * ATTN_SCALE, NEG)
        o_swa = _dot(_bf(_softmax_rows(sw)), vb)
        o_sel = acc_ref[rs, :] / l_ref[rs, :]
        lane0 = TAIL_GATES + (kvh * GQA + g) * 3
        gate = lambda br: jnp.sum(jnp.where(glane == lane0 + br, sg, 0.0), axis=-1, keepdims=True)
        o_ref[:, hs] = (gate(0) * o_cmp[rs] + gate(1) * o_sel + gate(2) * o_swa).astype(o_ref.dtype)


def _nsa_attn_prompt(qn, qr, tail, kc, vc, ks, vs, kw, vw, pe, cw, bsz, t):
    tq = NSA_TQ
    nq = t // tq
    ncb = t // CMP_BLOCK
    rows = GQA * tq
    qspec = pl.BlockSpec((tq, GQA * HEAD_DIM), lambda b, k, q: (b * nq + q, k))
    kvspec = pl.BlockSpec((t, HEAD_DIM), lambda b, k, q: (b, k))
    bf16 = jnp.bfloat16
    return pl.pallas_call(
        functools.partial(_nsa_attn_kernel, t=t, tq=tq),
        grid=(bsz, NSA_KV_HEADS, nq),
        in_specs=[qspec, qspec, pl.BlockSpec((tq, U_TAIL), lambda b, k, q: (b * nq + q, 0))] + [kvspec] * 6 + [
            pl.BlockSpec((2, CMP_BLOCK, HEAD_DIM), lambda b, k, q: (0, 0, 0)),
            pl.BlockSpec((2, HEAD_DIM, HEAD_DIM), lambda b, k, q: (0, 0, 0))],
        out_specs=qspec,
        out_shape=jax.ShapeDtypeStruct((bsz * t, NSA_DIM), bf16),
        scratch_shapes=[pltpu.VMEM((ncb, HEAD_DIM), bf16), pltpu.VMEM((ncb, HEAD_DIM), bf16),
                        pltpu.VMEM((ncb, HEAD_DIM), bf16)] + [pltpu.VMEM((t, HEAD_DIM), bf16)] * 4 + [
                        pltpu.VMEM((rows, 1), jnp.float32), pltpu.VMEM((rows, 1), jnp.float32),
                        pltpu.VMEM((rows, HEAD_DIM), jnp.float32)],
        compiler_params=_cparams(("parallel", "parallel", "arbitrary")),
        name="nsa_attn",
    )(qn, qr, tail, kc, vc, ks, vs, kw, vw, pe, cw)


NSA_PP = 16


def _nsa_sample_kernel(pt_ref, qn_ref, qr_ref, tail_ref, kcn_ref, vcn_ref, ksn_ref, vsn_ref, kwn_ref, vwn_ref,
                       swk_ref, swv_ref, pe_ref, cw_ref, *rest, tq, past_len, ns):
    pp = NSA_PP
    ck_refs, cv_refs = rest[0:pp], rest[pp:2 * pp]
    sk_refs, sv_refs = rest[2 * pp:3 * pp], rest[3 * pp:4 * pp]
    (o_ref, swk_out_ref, swv_out_ref,
     kcmp_ref, vcmp_ref, sel_ref, ocmp_ref, m_ref, l_ref, acc_ref) = rest[4 * pp:]
    f32 = jnp.float32
    kvh_n = NSA_KV_HEADS
    s = pl.program_id(1)
    rows = GQA * tq
    blk_per_page = PAGE_SIZE // CMP_BLOCK
    blk_rows = CMP_BLOCK * kvh_n
    step_rows = pp * blk_per_page * kvh_n
    ncb_past = past_len // CMP_BLOCK
    nsb = past_len // SEL_BLOCK + 1
    sel_shift = SEL_BLOCK.bit_length() - 1
    hd = lambda k: slice(k * HEAD_DIM, (k + 1) * HEAD_DIM)
    row8 = lax.broadcasted_iota(jnp.int32, (SUBLANES, HEAD_DIM), 0)

    def pe_mean(i):
        return jnp.mean(pe_ref[i], axis=0, keepdims=True)

    def stack_q(ref, k):
        return _bf(jnp.concatenate([ref[:, hd(k * GQA + g)] for g in range(GQA)], axis=0))

    def tile_rows(x):
        return jnp.concatenate([x] * GQA, axis=0)

    def head_rows(ref, k, n):
        return ref[pl.ds(k, n, stride=kvh_n), :]

    @pl.when(s == 0)
    def _():
        kcmp_ref[ncb_past * kvh_n:, :] = jnp.zeros((kcmp_ref.shape[0] - ncb_past * kvh_n, HEAD_DIM), f32)
        vcmp_ref[ncb_past * kvh_n:, :] = jnp.zeros((vcmp_ref.shape[0] - ncb_past * kvh_n, HEAD_DIM), f32)

    @pl.when(s < ns)
    def _():
        def block_sums(ref):
            sums = []
            for c in range(blk_per_page):
                x = ref[c * blk_rows:(c + 1) * blk_rows, :]
                s8 = jnp.sum(x.reshape(blk_rows // SUBLANES, SUBLANES, HEAD_DIM), axis=0)
                sums.append(s8 + pltpu.roll(s8, kvh_n, axis=0))
            return [jnp.where(row8 < kvh_n, sums[c], sums[c + 1]) for c in range(0, blk_per_page, 2)]

        ktiles, vtiles = [], []
        for r in range(pp):
            ktiles += block_sums(ck_refs[r])
            vtiles += block_sums(cv_refs[r])
        mk = jnp.concatenate(ktiles, axis=0) * (1.0 / CMP_BLOCK) + pe_mean(0)
        mv = jnp.concatenate(vtiles, axis=0) * (1.0 / CMP_BLOCK) + pe_mean(1)
        r0 = pl.multiple_of(s * step_rows, step_rows)
        kcmp_ref[pl.ds(r0, step_rows), :] = _dot(_bf(mk), _bf(cw_ref[0]))
        vcmp_ref[pl.ds(r0, step_rows), :] = _dot(_bf(mv), _bf(cw_ref[1]))

    @pl.when(s == ns - 1)
    def _():
        ncp = sel_ref.shape[1]

        def new_tile(new_ref, pe_i):
            pm = pe_mean(pe_i)
            m_new = jnp.sum(new_ref[...], axis=0, keepdims=True) * (1.0 / CMP_BLOCK)
            tile = jnp.where(row8 >= kvh_n, pm, 0.0)
            for k in range(kvh_n):
                tile = jnp.where(row8 == k, m_new[:, hd(k)] + pm, tile)
            return tile

        kcmp_ref[ncb_past * kvh_n:ncb_past * kvh_n + SUBLANES, :] = _dot(_bf(new_tile(kcn_ref, 0)), _bf(cw_ref[0]))
        vcmp_ref[ncb_past * kvh_n:ncb_past * kvh_n + SUBLANES, :] = _dot(_bf(new_tile(vcn_ref, 1)), _bf(cw_ref[1]))

        lane = lax.broadcasted_iota(jnp.int32, (rows, ncp), 1)
        qpos = past_len + (lax.broadcasted_iota(jnp.int32, (rows, ncp), 0) & (tq - 1))
        vis = (lane < ncb_past + 2) & ((lane + 1) * CMP_BLOCK - 1 <= qpos)
        anyvis = jnp.where(qpos[:, 0:1] >= CMP_BLOCK - 1, 1.0, 0.0).astype(f32)
        lane_t = lax.broadcasted_iota(jnp.int32, (tq, ncp), 1)
        imps = []
        for k in range(kvh_n):
            kk = _bf(head_rows(kcmp_ref, k, ncp))
            vv = _bf(head_rows(vcmp_ref, k, ncp))
            p = _softmax_rows(jnp.where(vis, _dot_nt(stack_q(qn_ref, k), kk) * ATTN_SCALE, NEG), divide=True) * anyvis
            ocmp_ref[k * rows:(k + 1) * rows, :] = _dot(_bf(p), vv)
            psum = p[0:tq]
            for g in range(1, GQA):
                psum = psum + p[g * tq:(g + 1) * tq]
            imps.append(psum + jnp.where((lane_t & 1) == 0, pltpu.roll(psum, ncp - 1, axis=1), pltpu.roll(psum, 1, axis=1)))
        imp = jnp.concatenate(imps, axis=0)
        nrow = kvh_n * tq
        j2 = lax.broadcasted_iota(jnp.int32, (nrow, ncp), 1)
        j = j2 >> 1
        real = ((j2 & 1) == 0) & (j < nsb)
        qp = past_len + (lax.broadcasted_iota(jnp.int32, (nrow, ncp), 0) & (tq - 1))
        forced = (j == (qp >> sel_shift)) | (j == 0)
        imp = jnp.where(forced, FORCE, jnp.where(j * SEL_BLOCK > qp, NEG, imp))
        imp = jnp.where(real, imp, 2.0 * NEG)
        cnt = jnp.zeros((nrow, ncp), f32)
        for i in range(nsb):
            ci = imp[:, 2 * i:2 * i + 1]
            cnt = cnt + jnp.where((ci > imp) | ((ci == imp) & (j2 > 2 * i)), 1.0, 0.0)
        sel_ref[...] = jnp.where((cnt < min(SEL_TOPK, nsb)) & real, 1.0, 0.0).astype(f32)
        m_ref[...] = jnp.full(m_ref.shape, NEG, f32)
        l_ref[...] = jnp.zeros(l_ref.shape, f32)
        acc_ref[...] = jnp.zeros(acc_ref.shape, f32)

    def online_update(k, sc, vb):
        rs = slice(k * rows, (k + 1) * rows)
        m_old = m_ref[rs, :]
        m_new = jnp.maximum(m_old, jnp.max(sc, axis=-1, keepdims=True))
        alpha = jnp.exp(m_old - m_new)
        pc = jnp.exp(sc - m_new)
        l_ref[rs, :] = alpha * l_ref[rs, :] + jnp.sum(pc, axis=-1, keepdims=True)
        acc_ref[rs, :] = alpha * acc_ref[rs, :] + _dot(_bf(pc), vb)
        m_ref[rs, :] = m_new

    @pl.when(s >= ns)
    def _():
        s2 = s - ns
        nkeys = pp * PAGE_SIZE
        ncp = sel_ref.shape[1]
        jrow = lax.broadcasted_iota(jnp.int32, (ncp, nkeys), 0)
        kblk = s2 * (nkeys // SEL_BLOCK) + (lax.broadcasted_iota(jnp.int32, (ncp, nkeys), 1) >> sel_shift)
        expand = jnp.where(jrow == 2 * kblk, 1.0, 0.0).astype(jnp.bfloat16)
        chosen = _dot(_bf(sel_ref[...]), expand)
        kpos = s2 * nkeys + lax.broadcasted_iota(jnp.int32, (tq, nkeys), 1)
        qpos = past_len + lax.broadcasted_iota(jnp.int32, (tq, nkeys), 0)
        for k in range(kvh_n):
            kcat = jnp.concatenate([head_rows(sk_refs[r], k, PAGE_SIZE) for r in range(pp)], axis=0)
            vcat = jnp.concatenate([head_rows(sv_refs[r], k, PAGE_SIZE) for r in range(pp)], axis=0)
            hidden = tile_rows(jnp.where((chosen[k * tq:(k + 1) * tq] > 0.5) & (kpos <= qpos), 0.0, 1.0))
            sc = jnp.where(hidden > 0.5, NEG, _dot_nt(stack_q(qr_ref, k), _bf(kcat)) * ATTN_SCALE)
            online_update(k, sc, _bf(vcat))

    @pl.when(s == 2 * ns - 1)
    def _():
        npad = 2 * SUBLANES
        zpad = jnp.zeros((npad - tq, KV_DIM), f32)
        pad_new = lambda ref: jnp.concatenate([ref[...], zpad], axis=0)
        ksn, vsn, kwn, vwn = pad_new(ksn_ref), pad_new(vsn_ref), pad_new(kwn_ref), pad_new(vwn_ref)
        win = swk_ref.shape[1]
        trow = lax.broadcasted_iota(jnp.int32, (rows, npad), 0) & (tq - 1)
        ncol = lax.broadcasted_iota(jnp.int32, (rows, npad), 1)
        new_vis = (ncol <= trow) & (ncol < tq)
        wdiff = (past_len + (lax.broadcasted_iota(jnp.int32, (rows, win), 0) & (tq - 1))) - (
            past_len - win + lax.broadcasted_iota(jnp.int32, (rows, win), 1))
        win_vis = (wdiff >= 0) & (wdiff <= WINDOW)
        sg = _sigmoid(tail_ref[...])
        for k in range(NSA_KV_HEADS):
            qr = stack_q(qr_ref, k)
            picked = tile_rows(sel_ref[k * tq:(k + 1) * tq, 2 * (nsb - 1):2 * (nsb - 1) + 1]) > 0.5
            sc = jnp.where(picked & new_vis, _dot_nt(qr, _bf(ksn[:, hd(k)])) * ATTN_SCALE, NEG)
            online_update(k, sc, _bf(vsn[:, hd(k)]))
            rs = slice(k * rows, (k + 1) * rows)
            o_sel = acc_ref[rs, :] / l_ref[rs, :]
            s1 = jnp.where(win_vis, _dot_nt(qr, _bf(swk_ref[0, :, hd(k)])) * ATTN_SCALE, NEG)
            s2n = jnp.where(new_vis, _dot_nt(qr, _bf(kwn[:, hd(k)])) * ATTN_SCALE, NEG)
            mx = jnp.maximum(jnp.max(s1, axis=-1, keepdims=True), jnp.max(s2n, axis=-1, keepdims=True))
            e1 = jnp.exp(s1 - mx)
            e2 = jnp.exp(s2n - mx)
            den = jnp.sum(e1, axis=-1, keepdims=True) + jnp.sum(e2, axis=-1, keepdims=True)
            o_swa = _dot(_bf(e1 / den), _bf(swv_ref[0, :, hd(k)])) + _dot(_bf(e2 / den), _bf(vwn[:, hd(k)]))
            o_cmp = ocmp_ref[rs, :]
            for g in range(GQA):
                c0 = TAIL_GATES + (k * GQA + g) * 3
                gs = slice(g * tq, (g + 1) * tq)
                o = (sg[:, c0:c0 + 1] * o_cmp[gs] + sg[:, c0 + 1:c0 + 2] * o_sel[gs] + sg[:, c0 + 2:c0 + 3] * o_swa[gs])
                o_ref[:, hd(k * GQA + g)] = o
        swk_out_ref[0, 0:win - tq, :] = swk_ref[0, tq:win, :]
        swk_out_ref[0, win - tq:win, :] = kwn_ref[...]
        swv_out_ref[0, 0:win - tq, :] = swv_ref[0, tq:win, :]
        swv_out_ref[0, win - tq:win, :] = vwn_ref[...]


def _nsa_sample(qn, qr, tail, kc, vc, ks, vs, kw, vw, swa_k, swa_v, cache_ck, cache_cv, cache_sk, cache_sv,
                page_table, pe, cw, bsz, tq, page0):
    n_pages = page_table.shape[1]
    past_len = n_pages * PAGE_SIZE
    pp = NSA_PP
    ns = n_pages // pp
    ncp = -(-(past_len // CMP_BLOCK + CMP_PER_SEL) // LANES) * LANES
    win = swa_k.shape[1]
    rows = GQA * tq
    page_rows = PAGE_SIZE * NSA_KV_HEADS
    row = lambda w: pl.BlockSpec((tq, w), lambda b, s, pt: (b, 0))
    state = pl.BlockSpec((1, win, KV_DIM), lambda b, s, pt: (b, 0, 0))
    full3 = lambda shp: pl.BlockSpec(shp, lambda b, s, pt: (0, 0, 0))

    def page(r, phase):
        if phase == 0:
            return pl.BlockSpec((page_rows, HEAD_DIM),
                                lambda b, s, pt: (page0 + pt[b, jnp.minimum(s, ns - 1) * pp + r], 0))
        return pl.BlockSpec((page_rows, HEAD_DIM),
                            lambda b, s, pt: (page0 + pt[b, jnp.maximum(s - ns, 0) * pp + r], 0))

    in_specs = ([row(NSA_DIM), row(NSA_DIM), row(U_TAIL)] + [row(KV_DIM)] * 6 + [state, state,
                full3((2, CMP_BLOCK, HEAD_DIM)), full3((2, HEAD_DIM, HEAD_DIM))]
                + [page(r, 0) for r in range(pp)] * 2 + [page(r, 1) for r in range(pp)] * 2)
    bf16, f32 = jnp.bfloat16, jnp.float32
    grid_spec = pltpu.PrefetchScalarGridSpec(
        num_scalar_prefetch=1,
        grid=(bsz, 2 * ns),
        in_specs=in_specs,
        out_specs=[row(NSA_DIM), state, state],
        scratch_shapes=[pltpu.VMEM((ncp * NSA_KV_HEADS, HEAD_DIM), f32)] * 2 + [
            pltpu.VMEM((NSA_KV_HEADS * tq, ncp), f32), pltpu.VMEM((NSA_KV_HEADS * rows, HEAD_DIM), f32),
            pltpu.VMEM((NSA_KV_HEADS * rows, 1), f32), pltpu.VMEM((NSA_KV_HEADS * rows, 1), f32),
            pltpu.VMEM((NSA_KV_HEADS * rows, HEAD_DIM), f32)])
    return pl.pallas_call(
        functools.partial(_nsa_sample_kernel, tq=tq, past_len=past_len, ns=ns),
        grid_spec=grid_spec,
        out_shape=[jax.ShapeDtypeStruct((bsz * tq, NSA_DIM), f32),
                   jax.ShapeDtypeStruct((bsz, win, KV_DIM), f32), jax.ShapeDtypeStruct((bsz, win, KV_DIM), f32)],
        compiler_params=_cparams(("parallel", "arbitrary")),
        name="nsa_sample",
    )(page_table, qn, qr, tail, kc, vc, ks, vs, kw, vw, swa_k, swa_v, pe, cw,
      *([cache_ck] * pp + [cache_cv] * pp + [cache_sk] * pp + [cache_sv] * pp))


def _rope_tables(pos, bsz):
    half = HEAD_DIM // 2
    inv = ROPE_THETA ** (-jnp.arange(half, dtype=jnp.float32) / half)
    ang = pos.astype(jnp.float32)[:, None] * inv[None, :]
    cos = jnp.cos(ang)
    sin = jnp.sin(ang)
    cos2 = jnp.concatenate([cos, cos], axis=-1)
    sin2 = jnp.concatenate([-sin, sin], axis=-1)
    return jnp.tile(cos2, (bsz, 1)), jnp.tile(sin2, (bsz, 1))


def _trunk_layer(x, p_emb, pos, lp, st, win_buf, layer):
    bsz, t, _ = x.shape
    m = bsz * t
    bf16, f32 = jnp.bfloat16, jnp.float32
    prompt = st is None
    act_dtype = bf16 if prompt else f32
    x2 = x.reshape(m, D_MODEL)
    h = _rmsnorm_bf16(x2, lp['attn_norm_g'])
    u_a = _matmul_w(h, lp['w_in'], layer, n_cols=U_A)
    u_b = _matmul(h, lp['w_in_b'])
    tail = _matmul(h, lp['w_in_tail'], tn=U_TAIL)
    u3 = u_a.reshape(bsz, t, U_A)
    tail3 = tail.reshape(bsz, t, U_TAIL)
    if prompt:
        buf_a = jnp.zeros((bsz, CONV_A_HALO, CONV_A_DIM), f32)
        buf_ssm = jnp.zeros((bsz, SSM_HALO, SSM_CONV_DIM), f32)
        h0 = jnp.zeros((bsz, SSM_DIM, SSM_STATE), f32)
        buf_f = None
    else:
        buf_a = jnp.pad(st['conv_a'], ((0, 0), (CONV_A_HALO - (CONV_A_WIDTH - 1), 0), (0, 0)))
        buf_ssm = jnp.pad(st['ssm_conv'], ((0, 0), (SSM_HALO - (SSM_CONV - 1), 0), (0, 0)))
        h0 = st['ssm'].reshape(bsz, SSM_DIM, SSM_STATE)
        buf_f = st['ffn_conv']

    a_out, nb_a = _conformer(u3, buf_a, lp['conv_a_w'], lp['conv_a_b'], lp['conv_a_ln_g'], lp['conv_a_ln_b'], act_dtype)
    new_a = nb_a[:, CONV_A_HALO - (CONV_A_WIDTH - 1):]

    if t % SSM_CHUNK:
        padt = ((0, 0), (0, SSM_CHUNK - t), (0, 0))
        u3s, tail3s, valid = jnp.pad(u3, padt), jnp.pad(tail3, padt), t
    else:
        u3s, tail3s, valid = u3, tail3, SSM_CHUNK
    pad_h = lambda v: jnp.pad(v.reshape(1, SSM_HEADS), ((0, 0), (0, LANES - SSM_HEADS)))
    b_out, h_new, nb_s = _ssd(u3s, tail3s, buf_ssm, h0, lp['ssm_conv_w'], lp['ssm_conv_b'].reshape(1, -1),
                              pad_h(lp['ssm_dt_bias']), pad_h(lp['ssm_a_log']),
                              jnp.repeat(lp['ssm_d'], SSM_HEADDIM).reshape(1, SSM_DIM),
                              lp['ssm_norm_g'].reshape(1, SSM_DIM), valid, bf16 if prompt else f32)
    b_out = b_out[:, :t]
    new_h = h_new.reshape(bsz, SSM_HEADS, SSM_HEADDIM, SSM_STATE)
    new_ssm_conv = nb_s[:, SSM_HALO - (SSM_CONV - 1):]

    cos, sin = _rope_tables(pos, bsz)
    qn, qr, kc, vc, ks, vs, kw, vw = _nsa_prep(u_b, cos, sin, lp['nsa_q_norm_g'], lp['nsa_k_norm_g'],
                                               bf16 if prompt else f32)
    kv4 = lambda v: v.reshape(bsz, -1, NSA_KV_HEADS, HEAD_DIM)
    if prompt:
        c_out = _nsa_attn_prompt(qn, qr, tail, kc, vc, ks, vs, kw, vw, lp['nsa_cmp_pe'], lp['nsa_cmp_w'], bsz, t)
        zpad = jnp.zeros((bsz, win_buf, NSA_KV_HEADS, HEAD_DIM), f32)
        kw_new = jnp.concatenate([zpad, kv4(kw)], axis=1)[:, t:]
        vw_new = jnp.concatenate([zpad, kv4(vw)], axis=1)[:, t:]
    else:
        flat3 = lambda v: v.reshape(-1, v.shape[-3], KV_DIM)
        rows2 = lambda v: v.reshape(-1, HEAD_DIM)
        n_phys = st['cmp_k'].shape[1]
        c_out, kw3, vw3 = _nsa_sample(qn, qr, tail, kc, vc, ks, vs, kw, vw, flat3(st['swa_k']), flat3(st['swa_v']),
                                      rows2(st['cmp_k']), rows2(st['cmp_v']), rows2(st['sel_k']), rows2(st['sel_v']),
                                      st['page_table'], lp['nsa_cmp_pe'], lp['nsa_cmp_w'], bsz, t, layer * n_phys)
        kw_new, vw_new = kv4(kw3), kv4(vw3)
    nsa_state = (kv4(kc), kv4(vc), kv4(ks), kv4(vs), kw_new, vw_new)

    mix = jnp.concatenate([a_out.reshape(m, -1).astype(bf16), b_out.reshape(m, -1).astype(bf16), c_out.astype(bf16)],
                          axis=-1)
    x2 = _matmul_w(mix, lp['w_out'], layer, residual=x2)
    h2 = _rmsnorm_bf16(x2, lp['ffn_norm_g'])
    if prompt:
        act, new_f = _ffn_up_act(h2, lp['w_up'], layer, lp['ffn_conv_w'], lp['ffn_conv_b'], bsz, t)
    else:
        gu = _matmul_w(h2, lp['w_up'], layer).reshape(bsz, t, 2 * D_FF)
        act, new_f = _ffn_act(gu, buf_f, lp['ffn_conv_w'], lp['ffn_conv_b'], act_dtype, nseq=bsz, tc=D_FF_HALF)
        act = act.reshape(m, D_FF).astype(bf16)
    x2 = _matmul_w(act, lp['w_down'], layer, residual=x2, tm=512, tk=D_FF_HALF, kblk=0)
    x2 = _matmul_w(act, lp['w_down'], layer, residual=x2, tm=512, tk=D_FF_HALF, kblk=1)
    h3 = _rmsnorm_bf16(x2, lp['ple_norm_g'])
    x2 = _ple(h3, lp['w_ple_gate'], p_emb.reshape(m, PLE_DIM), lp['w_ple_proj'], x2, layer)
    return x2.reshape(bsz, t, D_MODEL), nsa_state + (new_h, new_ssm_conv, new_a, new_f)


def _prep_weights(w_in):
    bf16 = jnp.bfloat16
    w_in_b = w_in[:, ORIG_Q:ORIG_GATES].astype(bf16)
    w_in_tail = jnp.concatenate([w_in[:, ORIG_DT:ORIG_Q], w_in[:, ORIG_GATES:],
                                 jnp.zeros((D_MODEL, U_TAIL - SSM_HEADS - 3 * NSA_HEADS), w_in.dtype)], axis=1).astype(bf16)
    return dict(w_in_b=w_in_b, w_in_tail=w_in_tail)


def kernel(x_prompt, x_sample, cache_cmp_k, cache_cmp_v, cache_sel_k, cache_sel_v, state_swa_k, state_swa_v,
           state_ssm, state_ssm_conv, state_conv_a, state_ffn_conv, page_table, p_prompt, p_sample,
           attn_norm_g, w_in, conv_a_w, conv_a_b, conv_a_ln_g, conv_a_ln_b, ssm_conv_w, ssm_conv_b,
           ssm_dt_bias, ssm_a_log, ssm_d, ssm_norm_g, nsa_q_norm_g, nsa_k_norm_g, nsa_cmp_pe, nsa_cmp_w,
           w_out, ffn_norm_g, w_up, ffn_conv_w, ffn_conv_b, w_down, ple_norm_g, w_ple_gate, w_ple_proj):
    past_len = page_table.shape[1] * PAGE_SIZE
    win_buf = state_swa_k.shape[2]
    pos_p = jnp.arange(x_prompt.shape[1], dtype=jnp.int32)
    pos_s = past_len + jnp.arange(x_sample.shape[1], dtype=jnp.int32)
    y_p, y_s = x_prompt, x_sample
    states_p, states_s = [], []
    for i in range(DEPTH):
        lp = {'attn_norm_g': attn_norm_g[i], 'conv_a_w': conv_a_w[i], 'conv_a_b': conv_a_b[i],
              'conv_a_ln_g': conv_a_ln_g[i], 'conv_a_ln_b': conv_a_ln_b[i], 'ssm_conv_w': ssm_conv_w[i],
              'ssm_conv_b': ssm_conv_b[i], 'ssm_dt_bias': ssm_dt_bias[i], 'ssm_a_log': ssm_a_log[i],
              'ssm_d': ssm_d[i], 'ssm_norm_g': ssm_norm_g[i], 'nsa_q_norm_g': nsa_q_norm_g[i],
              'nsa_k_norm_g': nsa_k_norm_g[i], 'nsa_cmp_pe': nsa_cmp_pe[i], 'nsa_cmp_w': nsa_cmp_w[i],
              'ffn_norm_g': ffn_norm_g[i], 'ple_norm_g': ple_norm_g[i],
              'w_in': w_in, 'w_out': w_out, 'w_up': w_up, 'w_down': w_down,
              'w_ple_gate': w_ple_gate, 'w_ple_proj': w_ple_proj,
              'ffn_conv_w': ffn_conv_w[i], 'ffn_conv_b': ffn_conv_b[i].reshape(1, D_FF)}
        lp.update(_prep_weights(w_in[i]))
        st = {'cmp_k': cache_cmp_k, 'cmp_v': cache_cmp_v, 'sel_k': cache_sel_k, 'sel_v': cache_sel_v,
              'swa_k': state_swa_k[i], 'swa_v': state_swa_v[i], 'ssm': state_ssm[i], 'ssm_conv': state_ssm_conv[i],
              'conv_a': state_conv_a[i], 'ffn_conv': state_ffn_conv[i], 'page_table': page_table}
        y_p, sp = _trunk_layer(y_p, p_prompt[i], pos_p, lp, None, win_buf, i)
        y_s, ss = _trunk_layer(y_s, p_sample[i], pos_s, lp, st, win_buf, i)
        states_p.append(sp)
        states_s.append(ss)
    (ck_p, cv_p, sk_p, sv_p, wk_p, wv_p, ssm_p, sc_p, ca_p, fc_p) = [jnp.stack(z) for z in zip(*states_p)]
    (ck_s, cv_s, sk_s, sv_s, wk_s, wv_s, ssm_s, sc_s, ca_s, fc_s) = [jnp.stack(z) for z in zip(*states_s)]
    return (y_p, y_s, ck_p, ck_s, cv_p, cv_s, sk_p, sk_s, sv_p, sv_s, wk_p, wk_s, wv_p, wv_s,
            ssm_p, ssm_s, sc_p, sc_s, ca_p, ca_s, fc_p, fc_s)
```

```python
import functools

import jax
import jax.numpy as jnp
from jax import lax
from jax.experimental import pallas as pl
from jax.experimental.pallas import tpu as pltpu

D_MODEL = 4096
DEPTH = 2
PAGE_SIZE = 128
PLE_DIM = 256
CONV_A_DIM = 1024
CONV_A_WIDTH = 31
SSM_DIM = 1024
SSM_HEADDIM = 64
SSM_HEADS = SSM_DIM // SSM_HEADDIM
SSM_GROUPS = 4
SSM_STATE = 128
SSM_CONV = 4
SSM_CHUNK = 128
SSM_CONV_DIM = SSM_DIM + 2 * SSM_GROUPS * SSM_STATE
NSA_HEADS = 16
NSA_KV_HEADS = 4
GQA = NSA_HEADS // NSA_KV_HEADS
HEAD_DIM = 128
NSA_DIM = NSA_HEADS * HEAD_DIM
KV_DIM = NSA_KV_HEADS * HEAD_DIM
CMP_BLOCK = 32
SEL_BLOCK = 64
CMP_PER_SEL = SEL_BLOCK // CMP_BLOCK
SEL_TOPK = 16
WINDOW = 512
SWA_QBLOCK = 128
ROPE_THETA = 10000.0
D_FF = 11008
FFN_CONV = 3
EPS = 1e-6
NEG = -1e30
FORCE = 1e9
ATTN_SCALE = HEAD_DIM ** -0.5

LANES = 128
SUBLANES = 8

OFF_CONV = 0
OFF_Z = OFF_CONV + 2 * CONV_A_DIM
OFF_XBC = OFF_Z + SSM_DIM
U_A = OFF_XBC + SSM_CONV_DIM
OFF_Q = 0
OFF_KV = OFF_Q + NSA_DIM
U_B = OFF_KV + 6 * KV_DIM
U_TAIL = LANES
TAIL_GATES = SSM_HEADS
ORIG_DT = U_A
ORIG_Q = ORIG_DT + SSM_HEADS
ORIG_GATES = ORIG_Q + U_B

D_FF_HALF = D_FF // 2
FFN_TC = 256
ROW_CHUNK = 256
CONV_A_HALO = 32
CONV_A_ROWS = 32
SSM_HALO = SUBLANES
NSA_TQ = 256
NSA_KC = 1024
VMEM_LIMIT = 54 * 1024 * 1024

_NT = (((1,), (1,)), ((), ()))


def _cparams(sem):
    return pltpu.CompilerParams(dimension_semantics=sem, vmem_limit_bytes=VMEM_LIMIT)


def _dot(a, b):
    return jnp.dot(a, b, preferred_element_type=jnp.float32)


def _dot_nt(a, b):
    return lax.dot_general(a, b, _NT, preferred_element_type=jnp.float32)


def _dot_f32(a, b):
    return jnp.dot(a, b, preferred_element_type=jnp.float32, precision=lax.Precision.HIGHEST)


def _bf(x):
    return x.astype(jnp.bfloat16)


def _sigmoid(x):
    return jax.nn.sigmoid(x)


def _rmsnorm_kernel(x_ref, g_ref, o_ref):
    x = x_ref[...]
    ms = jnp.mean(x * x, axis=-1, keepdims=True)
    o_ref[...] = (x * lax.rsqrt(ms + EPS) * g_ref[...]).astype(o_ref.dtype)


def _rmsnorm_bf16(x, g):
    m, d = x.shape
    tm = min(m, 256)
    return pl.pallas_call(
        _rmsnorm_kernel,
        grid=(m // tm,),
        in_specs=[pl.BlockSpec((tm, d), lambda i: (i, 0)), pl.BlockSpec((1, d), lambda i: (0, 0))],
        out_specs=pl.BlockSpec((tm, d), lambda i: (i, 0)),
        out_shape=jax.ShapeDtypeStruct((m, d), jnp.bfloat16),
        compiler_params=_cparams(("parallel",)),
        name="rmsnorm",
    )(x, g.reshape(1, d))


def _mm_kernel(a_ref, b_ref, *rest, nk, has_res):
    if has_res:
        r_ref, o_ref, acc_ref = rest
    else:
        o_ref, acc_ref = rest
    k = pl.program_id(2)
    part = _dot(a_ref[...], b_ref[...])

    if nk == 1:
        o_ref[...] = part + r_ref[...] if has_res else part
        return

    @pl.when(k == 0)
    def _():
        acc_ref[...] = part

    @pl.when(k > 0)
    def _():
        acc_ref[...] += part

    @pl.when(k == nk - 1)
    def _():
        o_ref[...] = acc_ref[...] + r_ref[...] if has_res else acc_ref[...]


def _matmul(a, b, residual=None, *, tm=1024, tn=512, tk=None):
    m, kdim = a.shape
    _, n = b.shape
    tm = min(tm, m)
    tn = min(tn, n)
    tk = kdim if tk is None else tk
    nk = kdim // tk
    in_specs = [pl.BlockSpec((tm, tk), lambda i, j, k: (i, k)), pl.BlockSpec((tk, tn), lambda i, j, k: (k, j))]
    args = [a, b]
    if residual is not None:
        in_specs.append(pl.BlockSpec((tm, tn), lambda i, j, k: (i, j)))
        args.append(residual)
    return pl.pallas_call(
        functools.partial(_mm_kernel, nk=nk, has_res=residual is not None),
        grid=(m // tm, n // tn, nk),
        in_specs=in_specs,
        out_specs=pl.BlockSpec((tm, tn), lambda i, j, k: (i, j)),
        out_shape=jax.ShapeDtypeStruct((m, n), jnp.float32),
        scratch_shapes=[pltpu.VMEM((tm, tn) if nk > 1 else (SUBLANES, LANES), jnp.float32)],
        compiler_params=_cparams(("parallel", "parallel", "arbitrary")),
        name="matmul",
    )(*args)


def _mmw_kernel(a_ref, w_ref, *rest, has_res):
    if has_res:
        r_ref, o_ref, wb_ref = rest
    else:
        o_ref, wb_ref = rest

    @pl.when(pl.program_id(1) == 0)
    def _():
        wb_ref[...] = _bf(w_ref[...])

    part = _dot(a_ref[...], wb_ref[...])
    o_ref[...] = part + r_ref[...] if has_res else part


def _matmul_w(a, w, layer, residual=None, *, n_cols=None, tm=1024, tn=512, tk=None, kblk=0):
    m = a.shape[0]
    tk = w.shape[1] if tk is None else tk
    n = w.shape[2] if n_cols is None else n_cols
    tm = min(tm, m)
    in_specs = [pl.BlockSpec((tm, tk), lambda j, i: (i, kblk)),
                pl.BlockSpec((None, tk, tn), lambda j, i: (layer, kblk, j))]
    args = [a, w]
    if residual is not None:
        in_specs.append(pl.BlockSpec((tm, tn), lambda j, i: (i, j)))
        args.append(residual)
    return pl.pallas_call(
        functools.partial(_mmw_kernel, has_res=residual is not None),
        grid=(n // tn, m // tm),
        in_specs=in_specs,
        out_specs=pl.BlockSpec((tm, tn), lambda j, i: (i, j)),
        out_shape=jax.ShapeDtypeStruct((m, n), jnp.float32),
        scratch_shapes=[pltpu.VMEM((tk, tn), jnp.bfloat16)],
        compiler_params=_cparams(("parallel", "arbitrary")),
        name="matmul_w",
    )(*args)


def _ple_kernel(h_ref, wg_ref, p_ref, wp_ref, x_ref, o_ref, wgb_ref, wpb_ref):
    @pl.when(pl.program_id(1) == 0)
    def _():
        wgb_ref[...] = _bf(wg_ref[...])
        wpb_ref[...] = _bf(wp_ref[...])

    gate = _sigmoid(_dot(h_ref[...], wgb_ref[...]))
    proj = _dot(_bf(p_ref[...]), wpb_ref[...])
    o_ref[...] = x_ref[...] + proj * gate


def _ple(h, wg, p, wp, x, layer, *, tm=1024, tn=512):
    m, d = h.shape
    n = wg.shape[2]
    tm = min(tm, m)
    return pl.pallas_call(
        _ple_kernel,
        grid=(n // tn, m // tm),
        in_specs=[pl.BlockSpec((tm, d), lambda j, i: (i, 0)), pl.BlockSpec((None, d, tn), lambda j, i: (layer, 0, j)),
                  pl.BlockSpec((tm, PLE_DIM), lambda j, i: (i, 0)),
                  pl.BlockSpec((None, PLE_DIM, tn), lambda j, i: (layer, 0, j)),
                  pl.BlockSpec((tm, tn), lambda j, i: (i, j))],
        out_specs=pl.BlockSpec((tm, tn), lambda j, i: (i, j)),
        out_shape=jax.ShapeDtypeStruct((m, n), jnp.float32),
        scratch_shapes=[pltpu.VMEM((d, tn), jnp.bfloat16), pltpu.VMEM((PLE_DIM, tn), jnp.bfloat16)],
        compiler_params=_cparams(("parallel", "arbitrary")),
        name="ple",
    )(h, wg, p, wp, x)


def _ffn_act_kernel(g_ref, u_ref, buf_ref, w_ref, b_ref, act_ref, nb_ref, *, nseq, t, rc):
    w = w_ref[...]
    bias = b_ref[...]
    for s in range(nseq):
        buf = buf_ref[s]

        def body(ci, carry, s=s, buf=buf):
            r0 = pl.multiple_of(ci * rc, rc)
            cur = g_ref[s, pl.ds(r0, rc), :]
            p0 = pl.multiple_of(jnp.maximum(r0 - SUBLANES, 0), SUBLANES)
            prev = g_ref[s, pl.ds(p0, SUBLANES), :]
            first = ci == 0
            hm1 = jnp.where(first, buf[1:2], prev[SUBLANES - 1:SUBLANES])
            hm2 = jnp.where(first, buf[0:1], prev[SUBLANES - 2:SUBLANES - 1])
            row = lax.broadcasted_iota(jnp.int32, cur.shape, 0)
            g1 = jnp.where(row == 0, hm1, pltpu.roll(cur, 1, axis=0))
            g2 = jnp.where(row == 0, hm2, jnp.where(row == 1, hm1, pltpu.roll(cur, 2, axis=0)))
            c = w[0:1] * g2 + w[1:2] * g1 + w[2:3] * cur + bias
            act_ref[s, pl.ds(r0, rc), :] = (c * _sigmoid(c) * u_ref[s, pl.ds(r0, rc), :]).astype(act_ref.dtype)
            return carry

        lax.fori_loop(0, t // rc, body, 0)
        nb_ref[s] = g_ref[s, t - 2:t, :]


def _ffn_act(gu, buf, w, b, out_dtype, *, nseq, tc):
    bsz, t, _ = gu.shape
    nc = D_FF // tc
    rc = min(t, ROW_CHUNK)
    return pl.pallas_call(
        functools.partial(_ffn_act_kernel, nseq=nseq, t=t, rc=rc),
        grid=(bsz // nseq, nc),
        in_specs=[pl.BlockSpec((nseq, t, tc), lambda bi, j: (bi, 0, j)),
                  pl.BlockSpec((nseq, t, tc), lambda bi, j: (bi, 0, j + nc)),
                  pl.BlockSpec((nseq, FFN_CONV - 1, tc), lambda bi, j: (bi, 0, j)),
                  pl.BlockSpec((FFN_CONV, tc), lambda bi, j: (0, j)),
                  pl.BlockSpec((1, tc), lambda bi, j: (0, j))],
        out_specs=[pl.BlockSpec((nseq, t, tc), lambda bi, j: (bi, 0, j)),
                   pl.BlockSpec((nseq, FFN_CONV - 1, tc), lambda bi, j: (bi, 0, j))],
        out_shape=[jax.ShapeDtypeStruct((bsz, t, D_FF), out_dtype),
                   jax.ShapeDtypeStruct((bsz, FFN_CONV - 1, D_FF), jnp.float32)],
        compiler_params=_cparams(("parallel", "parallel")),
        name="ffn_act",
    )(gu, gu, buf, w, b)


def _ffn_up_kernel(a_ref, wg_ref, wu_ref, cw_ref, cb_ref, act_ref, nf_ref, wb_ref, gu_ref, *, tm, tn, rc, tiles_per_seq):
    i = pl.program_id(1)

    @pl.when(i == 0)
    def _():
        wb_ref[:, 0:tn] = _bf(wg_ref[...])
        wb_ref[:, tn:2 * tn] = _bf(wu_ref[...])

    gu_ref[SUBLANES:SUBLANES + tm, :] = _dot(a_ref[...], wb_ref[...])
    seq_start = (i % tiles_per_seq) == 0
    w = cw_ref[...]
    bias = cb_ref[...]

    def body(ci, carry):
        r0 = pl.multiple_of(ci * rc, rc)
        cur = gu_ref[pl.ds(SUBLANES + r0, rc), 0:tn]
        prev = gu_ref[pl.ds(r0, SUBLANES), 0:tn]
        fresh = seq_start & (ci == 0)
        hm1 = jnp.where(fresh, 0.0, prev[SUBLANES - 1:SUBLANES])
        hm2 = jnp.where(fresh, 0.0, prev[SUBLANES - 2:SUBLANES - 1])
        row = lax.broadcasted_iota(jnp.int32, cur.shape, 0)
        g1 = jnp.where(row == 0, hm1, pltpu.roll(cur, 1, axis=0))
        g2 = jnp.where(row == 0, hm2, jnp.where(row == 1, hm1, pltpu.roll(cur, 2, axis=0)))
        c = w[0:1] * g2 + w[1:2] * g1 + w[2:3] * cur + bias
        act_ref[pl.ds(r0, rc), :] = (c * _sigmoid(c) * gu_ref[pl.ds(SUBLANES + r0, rc), tn:2 * tn]).astype(act_ref.dtype)
        return carry

    lax.fori_loop(0, tm // rc, body, 0)
    last = gu_ref[tm:tm + SUBLANES, 0:tn]
    nf_ref[0] = last[SUBLANES - (FFN_CONV - 1):SUBLANES]
    gu_ref[0:SUBLANES, 0:tn] = last


def _ffn_up_act(h, w_up, layer, cw, cb, bsz, t, *, tm=1024, tn=FFN_TC):
    m, d = h.shape
    nc = D_FF // tn
    tiles_per_seq = t // tm
    return pl.pallas_call(
        functools.partial(_ffn_up_kernel, tm=tm, tn=tn, rc=ROW_CHUNK, tiles_per_seq=tiles_per_seq),
        grid=(nc, m // tm),
        in_specs=[pl.BlockSpec((tm, d), lambda j, i: (i, 0)),
                  pl.BlockSpec((None, d, tn), lambda j, i: (layer, 0, j)),
                  pl.BlockSpec((None, d, tn), lambda j, i: (layer, 0, j + nc)),
                  pl.BlockSpec((FFN_CONV, tn), lambda j, i: (0, j)),
                  pl.BlockSpec((1, tn), lambda j, i: (0, j))],
        out_specs=[pl.BlockSpec((tm, tn), lambda j, i: (i, j)),
                   pl.BlockSpec((1, FFN_CONV - 1, tn), lambda j, i: (i // tiles_per_seq, 0, j))],
        out_shape=[jax.ShapeDtypeStruct((m, D_FF), jnp.bfloat16),
                   jax.ShapeDtypeStruct((bsz, FFN_CONV - 1, D_FF), jnp.float32)],
        scratch_shapes=[pltpu.VMEM((d, 2 * tn), jnp.bfloat16), pltpu.VMEM((SUBLANES + tm, 2 * tn), jnp.float32)],
        compiler_params=_cparams(("parallel", "arbitrary")),
        name="ffn_up_act",
    )(h, w_up, w_up, cw, cb)


def _conformer_kernel(u_ref, buf_ref, w_ref, b_ref, lg_ref, lb_ref, o_ref, nb_ref, ext_ref, *, tt, rc):
    ti = pl.program_id(1)

    @pl.when(ti == 0)
    def _():
        ext_ref[0:CONV_A_HALO, :] = buf_ref[0]

    x = u_ref[0]
    ext_ref[CONV_A_HALO:CONV_A_HALO + tt, :] = x[:, :CONV_A_DIM] * _sigmoid(x[:, CONV_A_DIM:])
    bias = b_ref[...]
    lg = lg_ref[...]
    lb = lb_ref[...]
    first_tap = CONV_A_HALO - (CONV_A_WIDTH - 1)
    for r in range(tt // rc):
        r0 = r * rc
        acc = jnp.zeros((rc, CONV_A_DIM), jnp.float32) + bias
        for k in range(CONV_A_WIDTH):
            acc = acc + w_ref[k:k + 1, :] * ext_ref[r0 + first_tap + k:r0 + first_tap + k + rc, :]
        mu = jnp.mean(acc, axis=-1, keepdims=True)
        d = acc - mu
        var = jnp.mean(d * d, axis=-1, keepdims=True)
        y = d * lax.rsqrt(var + EPS) * lg + lb
        o_ref[0, r0:r0 + rc, :] = (y * _sigmoid(y)).astype(o_ref.dtype)
    carry = ext_ref[tt:tt + CONV_A_HALO, :]
    nb_ref[0] = carry
    ext_ref[0:CONV_A_HALO, :] = carry


def _conformer(u3, buf, w, b, lg, lb, out_dtype):
    bsz, t, _ = u3.shape
    tt = min(t, ROW_CHUNK)
    rc = min(tt, CONV_A_ROWS)
    vec = lambda: pl.BlockSpec((1, CONV_A_DIM), lambda bi, ti: (0, 0))
    return pl.pallas_call(
        functools.partial(_conformer_kernel, tt=tt, rc=rc),
        grid=(bsz, t // tt),
        in_specs=[pl.BlockSpec((1, tt, 2 * CONV_A_DIM), lambda bi, ti: (bi, ti, OFF_CONV // (2 * CONV_A_DIM))),
                  pl.BlockSpec((1, CONV_A_HALO, CONV_A_DIM), lambda bi, ti: (bi, 0, 0)),
                  pl.BlockSpec((CONV_A_WIDTH, CONV_A_DIM), lambda bi, ti: (0, 0)),
                  vec(), vec(), vec()],
        out_specs=[pl.BlockSpec((1, tt, CONV_A_DIM), lambda bi, ti: (bi, ti, 0)),
                   pl.BlockSpec((1, CONV_A_HALO, CONV_A_DIM), lambda bi, ti: (bi, 0, 0))],
        out_shape=[jax.ShapeDtypeStruct((bsz, t, CONV_A_DIM), out_dtype),
                   jax.ShapeDtypeStruct((bsz, CONV_A_HALO, CONV_A_DIM), jnp.float32)],
        scratch_shapes=[pltpu.VMEM((CONV_A_HALO + tt, CONV_A_DIM), jnp.float32)],
        compiler_params=_cparams(("parallel", "arbitrary")),
        name="conformer",
    )(u3, buf, w, b.reshape(1, -1), lg.reshape(1, -1), lb.reshape(1, -1))


def _ssd_kernel(z_ref, xbc_lo_ref, xbc_hi_ref, dt_ref, cbuf_ref, h0_ref, cw_ref, cb_ref, dtb_ref, alog_ref, dskip_ref,
                ng_ref, y_ref, hout_ref, cbout_ref, ext_ref, h_ref, ys_ref, *, valid):
    f32 = jnp.float32
    ln = SSM_CHUNK
    ci = pl.program_id(1)

    @pl.when(ci == 0)
    def _():
        ext_ref[0:SSM_HALO, :] = cbuf_ref[0]
        h_ref[...] = h0_ref[0]

    ext_ref[SSM_HALO:SSM_HALO + ln, 0:SSM_CONV_DIM // 2] = xbc_lo_ref[0]
    ext_ref[SSM_HALO:SSM_HALO + ln, SSM_CONV_DIM // 2:SSM_CONV_DIM] = xbc_hi_ref[0]
    conv = jnp.zeros((ln, SSM_CONV_DIM), f32) + cb_ref[...]
    for k in range(SSM_CONV):
        s0 = SSM_HALO - (SSM_CONV - 1) + k
        conv = conv + cw_ref[k:k + 1, :] * ext_ref[s0:s0 + ln, :]
    c = conv * _sigmoid(conv)
    gn = SSM_GROUPS * SSM_STATE

    row = lax.broadcasted_iota(jnp.int32, (ln, LANES), 0)
    lane = lax.broadcasted_iota(jnp.int32, (ln, LANES), 1)
    xr = dt_ref[0] + dtb_ref[...]
    dt = jnp.maximum(xr, 0.0) + jnp.log1p(jnp.exp(-jnp.abs(xr)))
    if valid < ln:
        dt = jnp.where(row < valid, dt, 0.0)
    la = dt * (-jnp.exp(alog_ref[...]))
    tril = jnp.where(row >= lane, 1.0, 0.0).astype(f32)
    acum = _dot_f32(tril, la)
    acum_t = acum.T
    alast = acum[ln - 1:ln, :]
    causal = row >= lane
    lane_lo = lane < SSM_HEADDIM
    pair_w = 2 * SSM_HEADDIM

    cb_cache = {}
    for j in range(SSM_HEADS // 2):
        h0i, h1i = 2 * j, 2 * j + 1
        g = h0i // (SSM_HEADS // SSM_GROUPS)
        bm = _bf(c[:, SSM_DIM + g * SSM_STATE:SSM_DIM + (g + 1) * SSM_STATE])
        cm = _bf(c[:, SSM_DIM + gn + g * SSM_STATE:SSM_DIM + gn + (g + 1) * SSM_STATE])
        if g not in cb_cache:
            cb_cache[g] = _dot_nt(cm, bm)
        cbg = cb_cache[g]
        a0, a1 = acum[:, h0i:h0i + 1], acum[:, h1i:h1i + 1]
        dec0 = jnp.where(causal, jnp.exp(a0 - acum_t[h0i:h0i + 1, :]), 0.0)
        dec1 = jnp.where(causal, jnp.exp(a1 - acum_t[h1i:h1i + 1, :]), 0.0)
        sc = jnp.concatenate([_bf(cbg * dec0), _bf(cbg * dec1)], axis=1)
        xs = c[:, j * pair_w:(j + 1) * pair_w]
        xdt = xs * jnp.where(lane_lo, dt[:, h0i:h0i + 1], dt[:, h1i:h1i + 1])
        xblk = jnp.concatenate([_bf(jnp.where(lane_lo, xdt, 0.0)), _bf(jnp.where(lane_lo, 0.0, xdt))], axis=0)
        y_diag = _dot(sc, xblk)
        hp = h_ref[j * pair_w:(j + 1) * pair_w, :]
        y_off = _dot_nt(cm, _bf(hp)) * jnp.where(lane_lo, jnp.exp(a0), jnp.exp(a1))
        ys_ref[:, j * pair_w:(j + 1) * pair_w] = y_diag + y_off + xs * dskip_ref[:, j * pair_w:(j + 1) * pair_w]
        al0, al1 = alast[:, h0i:h0i + 1], alast[:, h1i:h1i + 1]
        dend = jnp.where(lane_lo, jnp.exp(al0 - a0), jnp.exp(al1 - a1))
        s_new = _dot(_bf((xdt * dend).T), bm)
        h_ref[j * pair_w:(j + 1) * pair_w, :] = jnp.where(row < SSM_HEADDIM, jnp.exp(al0), jnp.exp(al1)) * hp + s_new

    z = z_ref[0]
    y = ys_ref[...] * (z * _sigmoid(z))
    gw = SSM_DIM // SSM_GROUPS
    for g in range(SSM_GROUPS):
        yg = y[:, g * gw:(g + 1) * gw]
        ms = jnp.mean(yg * yg, axis=-1, keepdims=True)
        y_ref[0, :, g * gw:(g + 1) * gw] = (yg * lax.rsqrt(ms + EPS) * ng_ref[:, g * gw:(g + 1) * gw]).astype(y_ref.dtype)

    tail = ext_ref[valid:valid + SSM_HALO, :]
    cbout_ref[0] = tail
    ext_ref[0:SSM_HALO, :] = tail
    hout_ref[0] = h_ref[...]


def _ssd(u3, tail3, cbuf, h0, cw, cb, dtb, alog, dskip, ng, valid, out_dtype):
    bsz, t, _ = u3.shape
    ln = SSM_CHUNK
    half = SSM_CONV_DIM // 2
    full = lambda shape: pl.BlockSpec(shape, lambda bi, ci: (0,) * len(shape))
    return pl.pallas_call(
        functools.partial(_ssd_kernel, valid=valid),
        grid=(bsz, t // ln),
        in_specs=[pl.BlockSpec((1, ln, SSM_DIM), lambda bi, ci: (bi, ci, OFF_Z // SSM_DIM)),
                  pl.BlockSpec((1, ln, half), lambda bi, ci: (bi, ci, OFF_XBC // half)),
                  pl.BlockSpec((1, ln, half), lambda bi, ci: (bi, ci, OFF_XBC // half + 1)),
                  pl.BlockSpec((1, ln, U_TAIL), lambda bi, ci: (bi, ci, 0)),
                  pl.BlockSpec((1, SSM_HALO, SSM_CONV_DIM), lambda bi, ci: (bi, 0, 0)),
                  pl.BlockSpec((1, SSM_DIM, SSM_STATE), lambda bi, ci: (bi, 0, 0)),
                  full((SSM_CONV, SSM_CONV_DIM)), full((1, SSM_CONV_DIM)), full((1, LANES)), full((1, LANES)),
                  full((1, SSM_DIM)), full((1, SSM_DIM))],
        out_specs=[pl.BlockSpec((1, ln, SSM_DIM), lambda bi, ci: (bi, ci, 0)),
                   pl.BlockSpec((1, SSM_DIM, SSM_STATE), lambda bi, ci: (bi, 0, 0)),
                   pl.BlockSpec((1, SSM_HALO, SSM_CONV_DIM), lambda bi, ci: (bi, 0, 0))],
        out_shape=[jax.ShapeDtypeStruct((bsz, t, SSM_DIM), out_dtype),
                   jax.ShapeDtypeStruct((bsz, SSM_DIM, SSM_STATE), jnp.float32),
                   jax.ShapeDtypeStruct((bsz, SSM_HALO, SSM_CONV_DIM), jnp.float32)],
        scratch_shapes=[pltpu.VMEM((SSM_HALO + ln, SSM_CONV_DIM), jnp.float32),
                        pltpu.VMEM((SSM_DIM, SSM_STATE), jnp.float32),
                        pltpu.VMEM((ln, SSM_DIM), jnp.float32)],
        compiler_params=_cparams(("parallel", "arbitrary")),
        name="ssd",
    )(u3, u3, u3, tail3, cbuf, h0, cw, cb, dtb, alog, dskip, ng)


def _nsa_prep_kernel(q_ref, kvc_ref, kvs_ref, kvw_ref, cos_ref, sin_ref, qg_ref, kg_ref,
                     qn_ref, qr_ref, kc_ref, vc_ref, ks_ref, vs_ref, kw_ref, vw_ref):
    cos = cos_ref[...]
    sin = sin_ref[...]

    def norm(x, g):
        return x * lax.rsqrt(jnp.mean(x * x, axis=-1, keepdims=True) + EPS) * g

    def rope(x):
        return x * cos + pltpu.roll(x, HEAD_DIM // 2, axis=1) * sin

    qg = qg_ref[...]
    for h in range(NSA_HEADS):
        sl = slice(h * HEAD_DIM, (h + 1) * HEAD_DIM)
        x = norm(q_ref[:, sl], qg)
        qn_ref[:, sl] = x.astype(qn_ref.dtype)
        qr_ref[:, sl] = rope(x).astype(qr_ref.dtype)
    for h in range(NSA_KV_HEADS):
        sl = slice(h * HEAD_DIM, (h + 1) * HEAD_DIM)
        sv = slice(KV_DIM + h * HEAD_DIM, KV_DIM + (h + 1) * HEAD_DIM)
        kc_ref[:, sl] = norm(kvc_ref[:, sl], kg_ref[0:1, :])
        vc_ref[:, sl] = kvc_ref[:, sv]
        ks_ref[:, sl] = rope(norm(kvs_ref[:, sl], kg_ref[1:2, :]))
        vs_ref[:, sl] = kvs_ref[:, sv]
        kw_ref[:, sl] = rope(norm(kvw_ref[:, sl], kg_ref[2:3, :]))
        vw_ref[:, sl] = kvw_ref[:, sv]


def _nsa_prep(u, cos, sin, qg, kg, q_dtype):
    m = u.shape[0]
    tt = min(m, ROW_CHUNK)
    row = lambda w, blk=0: pl.BlockSpec((tt, w), lambda i: (i, blk))
    kv = jax.ShapeDtypeStruct((m, KV_DIM), jnp.float32)
    qo = jax.ShapeDtypeStruct((m, NSA_DIM), q_dtype)
    kv0 = OFF_KV // (2 * KV_DIM)
    return pl.pallas_call(
        _nsa_prep_kernel,
        grid=(m // tt,),
        in_specs=[row(NSA_DIM, OFF_Q // NSA_DIM), row(2 * KV_DIM, kv0), row(2 * KV_DIM, kv0 + 1), row(2 * KV_DIM, kv0 + 2),
                  row(HEAD_DIM), row(HEAD_DIM),
                  pl.BlockSpec((1, HEAD_DIM), lambda i: (0, 0)), pl.BlockSpec((3, HEAD_DIM), lambda i: (0, 0))],
        out_specs=[row(NSA_DIM), row(NSA_DIM)] + [row(KV_DIM)] * 6,
        out_shape=[qo, qo] + [kv] * 6,
        compiler_params=_cparams(("parallel",)),
        name="nsa_prep",
    )(u, u, u, u, cos, sin, qg.reshape(1, HEAD_DIM), kg)


def _softmax_rows(s, divide=False):
    m = jnp.max(s, axis=-1, keepdims=True)
    e = jnp.exp(s - m)
    den = jnp.sum(e, axis=-1, keepdims=True)
    if divide:
        return e / den
    return e * (1.0 / den)


def _nsa_attn_kernel(qn_ref, qr_ref, tail_ref, kc_ref, vc_ref, ks_ref, vs_ref, kw_ref, vw_ref, pe_ref, cw_ref,
                     o_ref, kcmp_ref, kcmpp_ref, vcmp_ref, ksb_ref, vsb_ref, kwb_ref, vwb_ref,
                     m_ref, l_ref, acc_ref, *, t, tq):
    f32 = jnp.float32
    kvh = pl.program_id(1)
    qi = pl.program_id(2)
    ncb = t // CMP_BLOCK
    nsb = t // SEL_BLOCK
    rows = GQA * tq
    span = min(WINDOW + tq, t)
    cmp_shift = CMP_BLOCK.bit_length() - 1
    sel_shift = SEL_BLOCK.bit_length() - 1

    @pl.when(qi == 0)
    def _():
        r = lax.broadcasted_iota(jnp.int32, (ncb, t), 0)
        cblk = lax.broadcasted_iota(jnp.int32, (ncb, t), 1) >> cmp_shift
        avg_nat = jnp.where(cblk == r, 1.0 / CMP_BLOCK, 0.0).astype(f32)
        perm = jnp.where(r < nsb, 2 * r, 2 * (r - nsb) + 1)
        avg_perm = jnp.where(cblk == perm, 1.0 / CMP_BLOCK, 0.0).astype(f32)
        kc = kc_ref[...]
        pe_k = jnp.mean(pe_ref[0], axis=0, keepdims=True)
        pe_v = jnp.mean(pe_ref[1], axis=0, keepdims=True)
        wk = _bf(cw_ref[0])
        wv = _bf(cw_ref[1])
        kcmp_ref[...] = _bf(_dot(_bf(_dot_f32(avg_nat, kc) + pe_k), wk))
        kcmpp_ref[...] = _bf(_dot(_bf(_dot_f32(avg_perm, kc) + pe_k), wk))
        vcmp_ref[...] = _bf(_dot(_bf(_dot_f32(avg_nat, vc_ref[...]) + pe_v), wv))
        ksb_ref[...] = _bf(ks_ref[...])
        vsb_ref[...] = _bf(vs_ref[...])
        kwb_ref[...] = _bf(kw_ref[...])
        vwb_ref[...] = _bf(vw_ref[...])

    t0 = qi * tq

    def stack(ref):
        return jnp.concatenate([ref[:, g * HEAD_DIM:(g + 1) * HEAD_DIM] for g in range(GQA)], axis=0)

    qn = stack(qn_ref)

    tpos_c = t0 + (lax.broadcasted_iota(jnp.int32, (rows, ncb), 0) & (tq - 1))
    blk_end = (lax.broadcasted_iota(jnp.int32, (rows, ncb), 1) + 1) * CMP_BLOCK - 1
    s = jnp.where(blk_end <= tpos_c, _dot_nt(qn, kcmp_ref[...]) * ATTN_SCALE, NEG)
    anyvis = jnp.where(tpos_c[:, 0:1] >= CMP_BLOCK - 1, 1.0, 0.0).astype(f32)
    p = _softmax_rows(s) * anyvis
    o_cmp = _dot(_bf(p), vcmp_ref[...])

    rperm = lax.broadcasted_iota(jnp.int32, (ncb, rows), 0)
    blk_t = jnp.where(rperm < nsb, 2 * rperm, 2 * (rperm - nsb) + 1)
    tpos_t = t0 + (lax.broadcasted_iota(jnp.int32, (ncb, rows), 1) & (tq - 1))
    st = jnp.where((blk_t + 1) * CMP_BLOCK - 1 <= tpos_t, _dot_nt(kcmpp_ref[...], qn) * ATTN_SCALE, NEG)
    mt = jnp.max(st, axis=0, keepdims=True)
    et = jnp.exp(st - mt)
    pt = et / jnp.sum(et, axis=0, keepdims=True) * jnp.where(tpos_t[0:1, :] >= CMP_BLOCK - 1, 1.0, 0.0).astype(f32)
    psum = pt[:, 0:tq]
    for g in range(1, GQA):
        psum = psum + pt[:, g * tq:(g + 1) * tq]
    imp = psum[0:nsb, :] + psum[nsb:2 * nsb, :]
    jrow = lax.broadcasted_iota(jnp.int32, (nsb, tq), 0)
    qp = t0 + lax.broadcasted_iota(jnp.int32, (nsb, tq), 1)
    forced = (jrow == (qp >> sel_shift)) | (jrow == 0)
    imp = jnp.where(forced, FORCE, jnp.where(jrow * SEL_BLOCK > qp, NEG, imp))
    cnt = jnp.zeros((nsb, tq), f32)
    for i in range(nsb):
        ri = imp[i:i + 1, :]
        cnt = cnt + jnp.where((ri > imp) | ((ri == imp) & (jrow > i)), 1.0, 0.0)
    sel_t = jnp.where(cnt < SEL_TOPK, 1.0, 0.0).astype(f32)
    sel_pad = jnp.concatenate([sel_t, jnp.zeros((LANES - nsb, tq), f32)], axis=0) if nsb < LANES else sel_t
    sel_q = _bf(sel_pad.T)

    kc_sz = min(NSA_KC, t)
    m_ref[...] = jnp.full((rows, 1), NEG, f32)
    l_ref[...] = jnp.zeros((rows, 1), f32)
    acc_ref[...] = jnp.zeros((rows, HEAD_DIM), f32)
    qpos_s = t0 + lax.broadcasted_iota(jnp.int32, (tq, kc_sz), 0)

    def sel_body(ck, carry):
        k0 = pl.multiple_of(ck * kc_sz, kc_sz)
        kb = ksb_ref[pl.ds(k0, kc_sz), :]
        vb = vsb_ref[pl.ds(k0, kc_sz), :]
        kpos = k0 + lax.broadcasted_iota(jnp.int32, (tq, kc_sz), 1)
        eblk = (k0 + lax.broadcasted_iota(jnp.int32, (LANES, kc_sz), 1)) >> sel_shift
        expand = jnp.where(eblk == lax.broadcasted_iota(jnp.int32, (LANES, kc_sz), 0), 1.0, 0.0).astype(jnp.bfloat16)
        chosen = _dot(sel_q, expand)
        visible = (chosen > 0.5) & (kpos <= qpos_s)
        for g in range(GQA):
            rs = slice(g * tq, (g + 1) * tq)
            sc = jnp.where(visible, _dot_nt(qr_ref[:, g * HEAD_DIM:(g + 1) * HEAD_DIM], kb) * ATTN_SCALE, NEG)
            m_old = m_ref[rs, :]
            m_new = jnp.maximum(m_old, jnp.max(sc, axis=-1, keepdims=True))
            alpha = jnp.exp(m_old - m_new)
            pc = jnp.exp(sc - m_new)
            l_ref[rs, :] = alpha * l_ref[rs, :] + jnp.sum(pc, axis=-1, keepdims=True)
            acc_ref[rs, :] = alpha * acc_ref[rs, :] + _dot(_bf(pc), vb)
            m_ref[rs, :] = m_new
        return carry

    lax.fori_loop(0, (t0 + tq + kc_sz - 1) // kc_sz, sel_body, 0)

    ws = pl.multiple_of(jnp.maximum(t0 + tq - span, 0), SWA_QBLOCK)
    kb = kwb_ref[pl.ds(ws, span), :]
    vb = vwb_ref[pl.ds(ws, span), :]
    diff = (t0 + lax.broadcasted_iota(jnp.int32, (tq, span), 0)) - (ws + lax.broadcasted_iota(jnp.int32, (tq, span), 1))
    win_vis = (diff >= 0) & (diff <= WINDOW)
    sg = _sigmoid(tail_ref[...])
    glane = lax.broadcasted_iota(jnp.int32, (tq, U_TAIL), 1)
    for g in range(GQA):
        rs = slice(g * tq, (g + 1) * tq)
        hs = slice(g * HEAD_DIM, (g + 1) * HEAD_DIM)
        sw = jnp.where(win_vis, _dot_nt(qr_ref[:, hs], kb) * ATTN_SCALE, NEG)
        o_swa = _dot(_bf(_softmax_rows(sw)), vb)
        o_sel = acc_ref[rs, :] / l_ref[rs, :]
        lane0 = TAIL_GATES + (kvh * GQA + g) * 3
        gate = lambda br: jnp.sum(jnp.where(glane == lane0 + br, sg, 0.0), axis=-1, keepdims=True)
        o_ref[:, hs] = (gate(0) * o_cmp[rs] + gate(1) * o_sel + gate(2) * o_swa).astype(o_ref.dtype)


def _nsa_attn_prompt(qn, qr, tail, kc, vc, ks, vs, kw, vw, pe, cw, bsz, t):
    tq = NSA_TQ
    nq = t // tq
    ncb = t // CMP_BLOCK
    rows = GQA * tq
    qspec = pl.BlockSpec((tq, GQA * HEAD_DIM), lambda b, k, q: (b * nq + q, k))
    kvspec = pl.BlockSpec((t, HEAD_DIM), lambda b, k, q: (b, k))
    bf16 = jnp.bfloat16
    return pl.pallas_call(
        functools.partial(_nsa_attn_kernel, t=t, tq=tq),
        grid=(bsz, NSA_KV_HEADS, nq),
        in_specs=[qspec, qspec, pl.BlockSpec((tq, U_TAIL), lambda b, k, q: (b * nq + q, 0))] + [kvspec] * 6 + [
            pl.BlockSpec((2, CMP_BLOCK, HEAD_DIM), lambda b, k, q: (0, 0, 0)),
            pl.BlockSpec((2, HEAD_DIM, HEAD_DIM), lambda b, k, q: (0, 0, 0))],
        out_specs=qspec,
        out_shape=jax.ShapeDtypeStruct((bsz * t, NSA_DIM), bf16),
        scratch_shapes=[pltpu.VMEM((ncb, HEAD_DIM), bf16), pltpu.VMEM((ncb, HEAD_DIM), bf16),
                        pltpu.VMEM((ncb, HEAD_DIM), bf16)] + [pltpu.VMEM((t, HEAD_DIM), bf16)] * 4 + [
                        pltpu.VMEM((rows, 1), jnp.float32), pltpu.VMEM((rows, 1), jnp.float32),
                        pltpu.VMEM((rows, HEAD_DIM), jnp.float32)],
        compiler_params=_cparams(("parallel", "parallel", "arbitrary")),
        name="nsa_attn",
    )(qn, qr, tail, kc, vc, ks, vs, kw, vw, pe, cw)


NSA_PP = 16


def _nsa_sample_kernel(pt_ref, qn_ref, qr_ref, tail_ref, kcn_ref, vcn_ref, ksn_ref, vsn_ref, kwn_ref, vwn_ref,
                       swk_ref, swv_ref, pe_ref, cw_ref, *rest, tq, past_len, ns):
    pp = NSA_PP
    ck_refs, cv_refs = rest[0:pp], rest[pp:2 * pp]
    sk_refs, sv_refs = rest[2 * pp:3 * pp], rest[3 * pp:4 * pp]
    (o_ref, swk_out_ref, swv_out_ref,
     kcmp_ref, vcmp_ref, sel_ref, ocmp_ref, m_ref, l_ref, acc_ref) = rest[4 * pp:]
    f32 = jnp.float32
    kvh_n = NSA_KV_HEADS
    s = pl.program_id(1)
    rows = GQA * tq
    blk_per_page = PAGE_SIZE // CMP_BLOCK
    blk_rows = CMP_BLOCK * kvh_n
    step_rows = pp * blk_per_page * kvh_n
    ncb_past = past_len // CMP_BLOCK
    nsb = past_len // SEL_BLOCK + 1
    sel_shift = SEL_BLOCK.bit_length() - 1
    hd = lambda k: slice(k * HEAD_DIM, (k + 1) * HEAD_DIM)
    row8 = lax.broadcasted_iota(jnp.int32, (SUBLANES, HEAD_DIM), 0)

    def pe_mean(i):
        return jnp.mean(pe_ref[i], axis=0, keepdims=True)

    def stack_q(ref, k):
        return _bf(jnp.concatenate([ref[:, hd(k * GQA + g)] for g in range(GQA)], axis=0))

    def tile_rows(x):
        return jnp.concatenate([x] * GQA, axis=0)

    def head_rows(ref, k, n):
        return ref[pl.ds(k, n, stride=kvh_n), :]

    @pl.when(s == 0)
    def _():
        kcmp_ref[ncb_past * kvh_n:, :] = jnp.zeros((kcmp_ref.shape[0] - ncb_past * kvh_n, HEAD_DIM), f32)
        vcmp_ref[ncb_past * kvh_n:, :] = jnp.zeros((vcmp_ref.shape[0] - ncb_past * kvh_n, HEAD_DIM), f32)

    @pl.when(s < ns)
    def _():
        def block_sums(ref):
            sums = []
            for c in range(blk_per_page):
                x = ref[c * blk_rows:(c + 1) * blk_rows, :]
                s8 = jnp.sum(x.reshape(blk_rows // SUBLANES, SUBLANES, HEAD_DIM), axis=0)
                sums.append(s8 + pltpu.roll(s8, kvh_n, axis=0))
            return [jnp.where(row8 < kvh_n, sums[c], sums[c + 1]) for c in range(0, blk_per_page, 2)]

        ktiles, vtiles = [], []
        for r in range(pp):
            ktiles += block_sums(ck_refs[r])
            vtiles += block_sums(cv_refs[r])
        mk = jnp.concatenate(ktiles, axis=0) * (1.0 / CMP_BLOCK) + pe_mean(0)
        mv = jnp.concatenate(vtiles, axis=0) * (1.0 / CMP_BLOCK) + pe_mean(1)
        r0 = pl.multiple_of(s * step_rows, step_rows)
        kcmp_ref[pl.ds(r0, step_rows), :] = _dot(_bf(mk), _bf(cw_ref[0]))
        vcmp_ref[pl.ds(r0, step_rows), :] = _dot(_bf(mv), _bf(cw_ref[1]))

    @pl.when(s == ns - 1)
    def _():
        ncp = sel_ref.shape[1]

        def new_tile(new_ref, pe_i):
            pm = pe_mean(pe_i)
            m_new = jnp.sum(new_ref[...], axis=0, keepdims=True) * (1.0 / CMP_BLOCK)
            tile = jnp.where(row8 >= kvh_n, pm, 0.0)
            for k in range(kvh_n):
                tile = jnp.where(row8 == k, m_new[:, hd(k)] + pm, tile)
            return tile

        kcmp_ref[ncb_past * kvh_n:ncb_past * kvh_n + SUBLANES, :] = _dot(_bf(new_tile(kcn_ref, 0)), _bf(cw_ref[0]))
        vcmp_ref[ncb_past * kvh_n:ncb_past * kvh_n + SUBLANES, :] = _dot(_bf(new_tile(vcn_ref, 1)), _bf(cw_ref[1]))

        lane = lax.broadcasted_iota(jnp.int32, (rows, ncp), 1)
        qpos = past_len + (lax.broadcasted_iota(jnp.int32, (rows, ncp), 0) & (tq - 1))
        vis = (lane < ncb_past + 2) & ((lane + 1) * CMP_BLOCK - 1 <= qpos)
        anyvis = jnp.where(qpos[:, 0:1] >= CMP_BLOCK - 1, 1.0, 0.0).astype(f32)
        lane_t = lax.broadcasted_iota(jnp.int32, (tq, ncp), 1)
        imps = []
        for k in range(kvh_n):
            kk = _bf(head_rows(kcmp_ref, k, ncp))
            vv = _bf(head_rows(vcmp_ref, k, ncp))
            p = _softmax_rows(jnp.where(vis, _dot_nt(stack_q(qn_ref, k), kk) * ATTN_SCALE, NEG), divide=True) * anyvis
            ocmp_ref[k * rows:(k + 1) * rows, :] = _dot(_bf(p), vv)
            psum = p[0:tq]
            for g in range(1, GQA):
                psum = psum + p[g * tq:(g + 1) * tq]
            imps.append(psum + jnp.where((lane_t & 1) == 0, pltpu.roll(psum, ncp - 1, axis=1), pltpu.roll(psum, 1, axis=1)))
        imp = jnp.concatenate(imps, axis=0)
        nrow = kvh_n * tq
        j2 = lax.broadcasted_iota(jnp.int32, (nrow, ncp), 1)
        j = j2 >> 1
        real = ((j2 & 1) == 0) & (j < nsb)
        qp = past_len + (lax.broadcasted_iota(jnp.int32, (nrow, ncp), 0) & (tq - 1))
        forced = (j == (qp >> sel_shift)) | (j == 0)
        imp = jnp.where(forced, FORCE, jnp.where(j * SEL_BLOCK > qp, NEG, imp))
        imp = jnp.where(real, imp, 2.0 * NEG)
        cnt = jnp.zeros((nrow, ncp), f32)
        for i in range(nsb):
            ci = imp[:, 2 * i:2 * i + 1]
            cnt = cnt + jnp.where((ci > imp) | ((ci == imp) & (j2 > 2 * i)), 1.0, 0.0)
        sel_ref[...] = jnp.where((cnt < min(SEL_TOPK, nsb)) & real, 1.0, 0.0).astype(f32)
        m_ref[...] = jnp.full(m_ref.shape, NEG, f32)
        l_ref[...] = jnp.zeros(l_ref.shape, f32)
        acc_ref[...] = jnp.zeros(acc_ref.shape, f32)

    def online_update(k, sc, vb):
        rs = slice(k * rows, (k + 1) * rows)
        m_old = m_ref[rs, :]
        m_new = jnp.maximum(m_old, jnp.max(sc, axis=-1, keepdims=True))
        alpha = jnp.exp(m_old - m_new)
        pc = jnp.exp(sc - m_new)
        l_ref[rs, :] = alpha * l_ref[rs, :] + jnp.sum(pc, axis=-1, keepdims=True)
        acc_ref[rs, :] = alpha * acc_ref[rs, :] + _dot(_bf(pc), vb)
        m_ref[rs, :] = m_new

    @pl.when(s >= ns)
    def _():
        s2 = s - ns
        nkeys = pp * PAGE_SIZE
        ncp = sel_ref.shape[1]
        jrow = lax.broadcasted_iota(jnp.int32, (ncp, nkeys), 0)
        kblk = s2 * (nkeys // SEL_BLOCK) + (lax.broadcasted_iota(jnp.int32, (ncp, nkeys), 1) >> sel_shift)
        expand = jnp.where(jrow == 2 * kblk, 1.0, 0.0).astype(jnp.bfloat16)
        chosen = _dot(_bf(sel_ref[...]), expand)
        kpos = s2 * nkeys + lax.broadcasted_iota(jnp.int32, (tq, nkeys), 1)
        qpos = past_len + lax.broadcasted_iota(jnp.int32, (tq, nkeys), 0)
        for k in range(kvh_n):
            kcat = jnp.concatenate([head_rows(sk_refs[r], k, PAGE_SIZE) for r in range(pp)], axis=0)
            vcat = jnp.concatenate([head_rows(sv_refs[r], k, PAGE_SIZE) for r in range(pp)], axis=0)
            hidden = tile_rows(jnp.where((chosen[k * tq:(k + 1) * tq] > 0.5) & (kpos <= qpos), 0.0, 1.0))
            sc = jnp.where(hidden > 0.5, NEG, _dot_nt(stack_q(qr_ref, k), _bf(kcat)) * ATTN_SCALE)
            online_update(k, sc, _bf(vcat))

    @pl.when(s == 2 * ns - 1)
    def _():
        npad = 2 * SUBLANES
        zpad = jnp.zeros((npad - tq, KV_DIM), f32)
        pad_new = lambda ref: jnp.concatenate([ref[...], zpad], axis=0)
        ksn, vsn, kwn, vwn = pad_new(ksn_ref), pad_new(vsn_ref), pad_new(kwn_ref), pad_new(vwn_ref)
        win = swk_ref.shape[1]
        trow = lax.broadcasted_iota(jnp.int32, (rows, npad), 0) & (tq - 1)
        ncol = lax.broadcasted_iota(jnp.int32, (rows, npad), 1)
        new_vis = (ncol <= trow) & (ncol < tq)
        wdiff = (past_len + (lax.broadcasted_iota(jnp.int32, (rows, win), 0) & (tq - 1))) - (
            past_len - win + lax.broadcasted_iota(jnp.int32, (rows, win), 1))
        win_vis = (wdiff >= 0) & (wdiff <= WINDOW)
        sg = _sigmoid(tail_ref[...])
        for k in range(NSA_KV_HEADS):
            qr = stack_q(qr_ref, k)
            picked = tile_rows(sel_ref[k * tq:(k + 1) * tq, 2 * (nsb - 1):2 * (nsb - 1) + 1]) > 0.5
            sc = jnp.where(picked & new_vis, _dot_nt(qr, _bf(ksn[:, hd(k)])) * ATTN_SCALE, NEG)
            online_update(k, sc, _bf(vsn[:, hd(k)]))
            rs = slice(k * rows, (k + 1) * rows)
            o_sel = acc_ref[rs, :] / l_ref[rs, :]
            s1 = jnp.where(win_vis, _dot_nt(qr, _bf(swk_ref[0, :, hd(k)])) * ATTN_SCALE, NEG)
            s2n = jnp.where(new_vis, _dot_nt(qr, _bf(kwn[:, hd(k)])) * ATTN_SCALE, NEG)
            mx = jnp.maximum(jnp.max(s1, axis=-1, keepdims=True), jnp.max(s2n, axis=-1, keepdims=True))
            e1 = jnp.exp(s1 - mx)
            e2 = jnp.exp(s2n - mx)
            den = jnp.sum(e1, axis=-1, keepdims=True) + jnp.sum(e2, axis=-1, keepdims=True)
            o_swa = _dot(_bf(e1 / den), _bf(swv_ref[0, :, hd(k)])) + _dot(_bf(e2 / den), _bf(vwn[:, hd(k)]))
            o_cmp = ocmp_ref[rs, :]
            for g in range(GQA):
                c0 = TAIL_GATES + (k * GQA + g) * 3
                gs = slice(g * tq, (g + 1) * tq)
                o = (sg[:, c0:c0 + 1] * o_cmp[gs] + sg[:, c0 + 1:c0 + 2] * o_sel[gs] + sg[:, c0 + 2:c0 + 3] * o_swa[gs])
                o_ref[:, hd(k * GQA + g)] = o
        swk_out_ref[0, 0:win - tq, :] = swk_ref[0, tq:win, :]
        swk_out_ref[0, win - tq:win, :] = kwn_ref[...]
        swv_out_ref[0, 0:win - tq, :] = swv_ref[0, tq:win, :]
        swv_out_ref[0, win - tq:win, :] = vwn_ref[...]


def _nsa_sample(qn, qr, tail, kc, vc, ks, vs, kw, vw, swa_k, swa_v, cache_ck, cache_cv, cache_sk, cache_sv,
                page_table, pe, cw, bsz, tq, page0):
    n_pages = page_table.shape[1]
    past_len = n_pages * PAGE_SIZE
    pp = NSA_PP
    ns = n_pages // pp
    ncp = -(-(past_len // CMP_BLOCK + CMP_PER_SEL) // LANES) * LANES
    win = swa_k.shape[1]
    rows = GQA * tq
    page_rows = PAGE_SIZE * NSA_KV_HEADS
    row = lambda w: pl.BlockSpec((tq, w), lambda b, s, pt: (b, 0))
    state = pl.BlockSpec((1, win, KV_DIM), lambda b, s, pt: (b, 0, 0))
    full3 = lambda shp: pl.BlockSpec(shp, lambda b, s, pt: (0, 0, 0))

    def page(r, phase):
        if phase == 0:
            return pl.BlockSpec((page_rows, HEAD_DIM),
                                lambda b, s, pt: (page0 + pt[b, jnp.minimum(s, ns - 1) * pp + r], 0))
        return pl.BlockSpec((page_rows, HEAD_DIM),
                            lambda b, s, pt: (page0 + pt[b, jnp.maximum(s - ns, 0) * pp + r], 0))

    in_specs = ([row(NSA_DIM), row(NSA_DIM), row(U_TAIL)] + [row(KV_DIM)] * 6 + [state, state,
                full3((2, CMP_BLOCK, HEAD_DIM)), full3((2, HEAD_DIM, HEAD_DIM))]
                + [page(r, 0) for r in range(pp)] * 2 + [page(r, 1) for r in range(pp)] * 2)
    bf16, f32 = jnp.bfloat16, jnp.float32
    grid_spec = pltpu.PrefetchScalarGridSpec(
        num_scalar_prefetch=1,
        grid=(bsz, 2 * ns),
        in_specs=in_specs,
        out_specs=[row(NSA_DIM), state, state],
        scratch_shapes=[pltpu.VMEM((ncp * NSA_KV_HEADS, HEAD_DIM), f32)] * 2 + [
            pltpu.VMEM((NSA_KV_HEADS * tq, ncp), f32), pltpu.VMEM((NSA_KV_HEADS * rows, HEAD_DIM), f32),
            pltpu.VMEM((NSA_KV_HEADS * rows, 1), f32), pltpu.VMEM((NSA_KV_HEADS * rows, 1), f32),
            pltpu.VMEM((NSA_KV_HEADS * rows, HEAD_DIM), f32)])
    return pl.pallas_call(
        functools.partial(_nsa_sample_kernel, tq=tq, past_len=past_len, ns=ns),
        grid_spec=grid_spec,
        out_shape=[jax.ShapeDtypeStruct((bsz * tq, NSA_DIM), f32),
                   jax.ShapeDtypeStruct((bsz, win, KV_DIM), f32), jax.ShapeDtypeStruct((bsz, win, KV_DIM), f32)],
        compiler_params=_cparams(("parallel", "arbitrary")),
        name="nsa_sample",
    )(page_table, qn, qr, tail, kc, vc, ks, vs, kw, vw, swa_k, swa_v, pe, cw,
      *([cache_ck] * pp + [cache_cv] * pp + [cache_sk] * pp + [cache_sv] * pp))


def _rope_tables(pos, bsz):
    half = HEAD_DIM // 2
    inv = ROPE_THETA ** (-jnp.arange(half, dtype=jnp.float32) / half)
    ang = pos.astype(jnp.float32)[:, None] * inv[None, :]
    cos = jnp.cos(ang)
    sin = jnp.sin(ang)
    cos2 = jnp.concatenate([cos, cos], axis=-1)
    sin2 = jnp.concatenate([-sin, sin], axis=-1)
    return jnp.tile(cos2, (bsz, 1)), jnp.tile(sin2, (bsz, 1))


def _trunk_layer(x, p_emb, pos, lp, st, win_buf, layer):
    bsz, t, _ = x.shape
    m = bsz * t
    bf16, f32 = jnp.bfloat16, jnp.float32
    prompt = st is None
    act_dtype = bf16 if prompt else f32
    x2 = x.reshape(m, D_MODEL)
    h = _rmsnorm_bf16(x2, lp['attn_norm_g'])
    u_a = _matmul(h, lp['w_in_a'])
    u_b = _matmul(h, lp['w_in_b'])
    tail = _matmul(h, lp['w_in_tail'], tn=U_TAIL)
    u3 = u_a.reshape(bsz, t, U_A)
    tail3 = tail.reshape(bsz, t, U_TAIL)
    if prompt:
        buf_a = jnp.zeros((bsz, CONV_A_HALO, CONV_A_DIM), f32)
        buf_ssm = jnp.zeros((bsz, SSM_HALO, SSM_CONV_DIM), f32)
        h0 = jnp.zeros((bsz, SSM_DIM, SSM_STATE), f32)
        buf_f = None
    else:
        buf_a = jnp.pad(st['conv_a'], ((0, 0), (CONV_A_HALO - (CONV_A_WIDTH - 1), 0), (0, 0)))
        buf_ssm = jnp.pad(st['ssm_conv'], ((0, 0), (SSM_HALO - (SSM_CONV - 1), 0), (0, 0)))
        h0 = st['ssm'].reshape(bsz, SSM_DIM, SSM_STATE)
        buf_f = st['ffn_conv']

    a_out, nb_a = _conformer(u3, buf_a, lp['conv_a_w'], lp['conv_a_b'], lp['conv_a_ln_g'], lp['conv_a_ln_b'], act_dtype)
    new_a = nb_a[:, CONV_A_HALO - (CONV_A_WIDTH - 1):]

    if t % SSM_CHUNK:
        padt = ((0, 0), (0, SSM_CHUNK - t), (0, 0))
        u3s, tail3s, valid = jnp.pad(u3, padt), jnp.pad(tail3, padt), t
    else:
        u3s, tail3s, valid = u3, tail3, SSM_CHUNK
    pad_h = lambda v: jnp.pad(v.reshape(1, SSM_HEADS), ((0, 0), (0, LANES - SSM_HEADS)))
    b_out, h_new, nb_s = _ssd(u3s, tail3s, buf_ssm, h0, lp['ssm_conv_w'], lp['ssm_conv_b'].reshape(1, -1),
                              pad_h(lp['ssm_dt_bias']), pad_h(lp['ssm_a_log']),
                              jnp.repeat(lp['ssm_d'], SSM_HEADDIM).reshape(1, SSM_DIM),
                              lp['ssm_norm_g'].reshape(1, SSM_DIM), valid, bf16 if prompt else f32)
    b_out = b_out[:, :t]
    new_h = h_new.reshape(bsz, SSM_HEADS, SSM_HEADDIM, SSM_STATE)
    new_ssm_conv = nb_s[:, SSM_HALO - (SSM_CONV - 1):]

    cos, sin = _rope_tables(pos, bsz)
    qn, qr, kc, vc, ks, vs, kw, vw = _nsa_prep(u_b, cos, sin, lp['nsa_q_norm_g'], lp['nsa_k_norm_g'],
                                               bf16 if prompt else f32)
    kv4 = lambda v: v.reshape(bsz, -1, NSA_KV_HEADS, HEAD_DIM)
    if prompt:
        c_out = _nsa_attn_prompt(qn, qr, tail, kc, vc, ks, vs, kw, vw, lp['nsa_cmp_pe'], lp['nsa_cmp_w'], bsz, t)
        zpad = jnp.zeros((bsz, win_buf, NSA_KV_HEADS, HEAD_DIM), f32)
        kw_new = jnp.concatenate([zpad, kv4(kw)], axis=1)[:, t:]
        vw_new = jnp.concatenate([zpad, kv4(vw)], axis=1)[:, t:]
    else:
        flat3 = lambda v: v.reshape(-1, v.shape[-3], KV_DIM)
        rows2 = lambda v: v.reshape(-1, HEAD_DIM)
        n_phys = st['cmp_k'].shape[1]
        c_out, kw3, vw3 = _nsa_sample(qn, qr, tail, kc, vc, ks, vs, kw, vw, flat3(st['swa_k']), flat3(st['swa_v']),
                                      rows2(st['cmp_k']), rows2(st['cmp_v']), rows2(st['sel_k']), rows2(st['sel_v']),
                                      st['page_table'], lp['nsa_cmp_pe'], lp['nsa_cmp_w'], bsz, t, layer * n_phys)
        kw_new, vw_new = kv4(kw3), kv4(vw3)
    nsa_state = (kv4(kc), kv4(vc), kv4(ks), kv4(vs), kw_new, vw_new)

    mix = jnp.concatenate([a_out.reshape(m, -1).astype(bf16), b_out.reshape(m, -1).astype(bf16), c_out.astype(bf16)],
                          axis=-1)
    x2 = _matmul_w(mix, lp['w_out'], layer, residual=x2)
    h2 = _rmsnorm_bf16(x2, lp['ffn_norm_g'])
    if prompt:
        act, new_f = _ffn_up_act(h2, lp['w_up'], layer, lp['ffn_conv_w'], lp['ffn_conv_b'], bsz, t)
    else:
        gu = _matmul_w(h2, lp['w_up'], layer).reshape(bsz, t, 2 * D_FF)
        act, new_f = _ffn_act(gu, buf_f, lp['ffn_conv_w'], lp['ffn_conv_b'], act_dtype, nseq=bsz, tc=D_FF_HALF)
        act = act.reshape(m, D_FF).astype(bf16)
    x2 = _matmul_w(act, lp['w_down'], layer, residual=x2, tm=512, tk=D_FF_HALF, kblk=0)
    x2 = _matmul_w(act, lp['w_down'], layer, residual=x2, tm=512, tk=D_FF_HALF, kblk=1)
    h3 = _rmsnorm_bf16(x2, lp['ple_norm_g'])
    x2 = _ple(h3, lp['w_ple_gate'], p_emb.reshape(m, PLE_DIM), lp['w_ple_proj'], x2, layer)
    return x2.reshape(bsz, t, D_MODEL), nsa_state + (new_h, new_ssm_conv, new_a, new_f)


def _prep_weights(w_in):
    bf16 = jnp.bfloat16
    w_in_a = w_in[:, :U_A].astype(bf16)
    w_in_b = w_in[:, ORIG_Q:ORIG_GATES].astype(bf16)
    w_in_tail = jnp.concatenate([w_in[:, ORIG_DT:ORIG_Q], w_in[:, ORIG_GATES:],
                                 jnp.zeros((D_MODEL, U_TAIL - SSM_HEADS - 3 * NSA_HEADS), w_in.dtype)], axis=1).astype(bf16)
    return dict(w_in_a=w_in_a, w_in_b=w_in_b, w_in_tail=w_in_tail)


def kernel(x_prompt, x_sample, cache_cmp_k, cache_cmp_v, cache_sel_k, cache_sel_v, state_swa_k, state_swa_v,
           state_ssm, state_ssm_conv, state_conv_a, state_ffn_conv, page_table, p_prompt, p_sample,
           attn_norm_g, w_in, conv_a_w, conv_a_b, conv_a_ln_g, conv_a_ln_b, ssm_conv_w, ssm_conv_b,
           ssm_dt_bias, ssm_a_log, ssm_d, ssm_norm_g, nsa_q_norm_g, nsa_k_norm_g, nsa_cmp_pe, nsa_cmp_w,
           w_out, ffn_norm_g, w_up, ffn_conv_w, ffn_conv_b, w_down, ple_norm_g, w_ple_gate, w_ple_proj):
    past_len = page_table.shape[1] * PAGE_SIZE
    win_buf = state_swa_k.shape[2]
    pos_p = jnp.arange(x_prompt.shape[1], dtype=jnp.int32)
    pos_s = past_len + jnp.arange(x_sample.shape[1], dtype=jnp.int32)
    y_p, y_s = x_prompt, x_sample
    states_p, states_s = [], []
    for i in range(DEPTH):
        lp = {'attn_norm_g': attn_norm_g[i], 'conv_a_w': conv_a_w[i], 'conv_a_b': conv_a_b[i],
              'conv_a_ln_g': conv_a_ln_g[i], 'conv_a_ln_b': conv_a_ln_b[i], 'ssm_conv_w': ssm_conv_w[i],
              'ssm_conv_b': ssm_conv_b[i], 'ssm_dt_bias': ssm_dt_bias[i], 'ssm_a_log': ssm_a_log[i],
              'ssm_d': ssm_d[i], 'ssm_norm_g': ssm_norm_g[i], 'nsa_q_norm_g': nsa_q_norm_g[i],
              'nsa_k_norm_g': nsa_k_norm_g[i], 'nsa_cmp_pe': nsa_cmp_pe[i], 'nsa_cmp_w': nsa_cmp_w[i],
              'ffn_norm_g': ffn_norm_g[i], 'ple_norm_g': ple_norm_g[i],
              'w_in': w_in, 'w_out': w_out, 'w_up': w_up, 'w_down': w_down,
              'w_ple_gate': w_ple_gate, 'w_ple_proj': w_ple_proj,
              'ffn_conv_w': ffn_conv_w[i], 'ffn_conv_b': ffn_conv_b[i].reshape(1, D_FF)}
        lp.update(_prep_weights(w_in[i]))
        st = {'cmp_k': cache_cmp_k, 'cmp_v': cache_cmp_v, 'sel_k': cache_sel_k, 'sel_v': cache_sel_v,
              'swa_k': state_swa_k[i], 'swa_v': state_swa_v[i], 'ssm': state_ssm[i], 'ssm_conv': state_ssm_conv[i],
              'conv_a': state_conv_a[i], 'ffn_conv': state_ffn_conv[i], 'page_table': page_table}
        y_p, sp = _trunk_layer(y_p, p_prompt[i], pos_p, lp, None, win_buf, i)
        y_s, ss = _trunk_layer(y_s, p_sample[i], pos_s, lp, st, win_buf, i)
        states_p.append(sp)
        states_s.append(ss)
    (ck_p, cv_p, sk_p, sv_p, wk_p, wv_p, ssm_p, sc_p, ca_p, fc_p) = [jnp.stack(z) for z in zip(*states_p)]
    (ck_s, cv_s, sk_s, sv_s, wk_s, wv_s, ssm_s, sc_s, ca_s, fc_s) = [jnp.stack(z) for z in zip(*states_s)]
    return (y_p, y_s, ck_p, ck_s, cv_p, cv_s, sk_p, sk_s, sv_p, sv_s, wk_p, wk_s, wv_p, wv_s,
            ssm_p, ssm_s, sc_p, sc_s, ca_p, ca_s, fc_p, fc_s)
```
